```python
import numpy as np
import jax
import jax.numpy as jnp
from jax import lax

D_MODEL = 1024
BATCH = 2
SEQ = 16384
DEPTH = 4

HEAD_DIM = 64
RET_HEADS = 8
RWKV_HEADS = 8
RET_WIDTH = RET_HEADS * HEAD_DIM
RWKV_WIDTH = RWKV_HEADS * HEAD_DIM
MIX_WIDTH = RET_WIDTH + RWKV_WIDTH
RET_CHUNK = 128
ROPE_BASE = 10000.0
RET_GN_EPS = 1e-6
W_LORA = 64
A_LORA = 64
G_LORA = 128
RWKV_GN_EPS = 64e-5
RWKV_SHIFT_WIDTH = 3 * RWKV_WIDTH + W_LORA + A_LORA + G_LORA
EVEN_IN = 4 * RET_WIDTH + RWKV_SHIFT_WIDTH
SWA_HEADS = 16
SWA_KV_HEADS = 4
SWA_GROUP = SWA_HEADS // SWA_KV_HEADS
SWA_WINDOW = 128
SWA_WIDTH = SWA_HEADS * HEAD_DIM
SWA_QKV = (SWA_HEADS + 2 * SWA_KV_HEADS) * HEAD_DIM
D_FF = -(-8 * D_MODEL // (3 * 256)) * 256
RMS_EPS = 1e-6
N_EVEN = (DEPTH + 1) // 2
N_ODD = DEPTH // 2

kernel_name = 'hybrid_retention_rwkv7_swa_sink_trunk'


def rms_norm(x, g):
    xf = x.astype(jnp.float32)
    y = xf * lax.rsqrt(jnp.mean(xf * xf, axis=-1, keepdims=True) + RMS_EPS)
    return (y * g.astype(jnp.float32)).astype(x.dtype)


def head_group_norm(y, eps, gain=None, bias=None):
    yf = y.astype(jnp.float32)
    mu = jnp.mean(yf, axis=-1, keepdims=True)
    var = jnp.mean(jnp.square(yf - mu), axis=-1, keepdims=True)
    out = (yf - mu) * lax.rsqrt(var + eps)
    if gain is not None:
        out = out * gain.astype(jnp.float32) + bias.astype(jnp.float32)
    return out.astype(y.dtype)


def rotary(x, pos):
    half = x.shape[-1] // 2
    inv_freq = ROPE_BASE ** (-jnp.linspace(0.0, 1.0, half))
    ang = pos[:, None] * inv_freq[None, :]
    cos = jnp.cos(ang)[None, :, None, :].astype(x.dtype)
    sin = jnp.sin(ang)[None, :, None, :].astype(x.dtype)
    x1, x2 = x[..., :half], x[..., half:]
    return jnp.concatenate([x1 * cos - x2 * sin, x1 * sin + x2 * cos], axis=-1)


def token_shift(z):
    return jnp.pad(z, ((0, 0), (1, 0), (0, 0)))[:, :-1]


def chunk_retention(q, k, v):
    B, T, H, D = q.shape
    C = RET_CHUNK
    N = T // C
    log_gamma = jnp.log1p(-(2.0 ** (-5.0 - jnp.arange(H, dtype=jnp.float32))))
    q = q.reshape(B, N, C, H, D)
    k = k.reshape(B, N, C, H, D)
    v = v.reshape(B, N, C, H, D)
    idx = jnp.arange(C, dtype=jnp.float32)
    rel = idx[:, None] - idx[None, :]
    inner_decay = jnp.where(rel >= 0, jnp.exp(jnp.maximum(rel, 0.0)[None] * log_gamma[:, None, None]), 0.0)
    scores = jnp.einsum('bnihd,bnjhd->bnhij', q, k) * inner_decay.astype(q.dtype)
    o_inner = jnp.einsum('bnhij,bnjhd->bnihd', scores, v)
    k_dec = jnp.exp((C - 1 - idx)[:, None] * log_gamma[None, :]).astype(q.dtype)
    kv = jnp.einsum('bnjhd,bnjhe->nbhde', k * k_dec[:, :, None], v)
    chunk_decay = jnp.exp(C * log_gamma)[None, :, None, None].astype(kv.dtype)

    def step(state, kv_c):
        return chunk_decay * state + kv_c, state

    _, s_prev = lax.scan(step, jnp.zeros_like(kv[0]), kv)
    q_dec = jnp.exp((idx + 1.0)[:, None] * log_gamma[None, :]).astype(q.dtype)
    o_cross = jnp.einsum('bnihd,nbhde->bnihe', q * q_dec[:, :, None], s_prev)
    return (o_inner + o_cross).reshape(B, T, H, D)


def rwkv7_scan(r, w, k, v, a, b):
    B, T, H, D = r.shape

    def step(state, inp):
        r_t, w_t, k_t, v_t, a_t, b_t = inp
        sa = jnp.einsum('bhij,bhj->bhi', state, a_t)
        state = state * w_t[:, :, None, :] + sa[..., None] * b_t[:, :, None, :] + v_t[..., None] * k_t[:, :, None, :]
        return state, jnp.einsum('bhij,bhj->bhi', state, r_t)

    xs = tuple(jnp.moveaxis(t, 1, 0) for t in (r, w, k, v, a, b))
    _, y = lax.scan(step, jnp.zeros((B, H, D, D), r.dtype), xs)
    return jnp.moveaxis(y, 0, 1)


def retention_rwkv_mixer(h, w_in, w_out, mu, w0, w_up, a0, a_up, g_up, k_k, k_a, r_k, ln_g, ln_b):
    B, T, _ = h.shape

    def heads(t):
        return t.reshape(B, T, -1, HEAD_DIM)

    z = h @ w_in
    z_ret, z_rwkv = z[..., :4 * RET_WIDTH], z[..., 4 * RET_WIDTH:]

    q, k, v, g = jnp.split(z_ret, 4, axis=-1)
    pos = jnp.arange(T, dtype=jnp.float32)
    q = rotary(heads(q), pos)
    k = rotary(heads(k), pos) * HEAD_DIM ** -0.5
    o = chunk_retention(q, k, heads(v))
    ret_out = head_group_norm(o, RET_GN_EPS).reshape(B, T, RET_WIDTH) * jax.nn.silu(g)

    zs = z_rwkv + mu * (token_shift(z_rwkv) - z_rwkv)
    split_at = [RWKV_WIDTH, 2 * RWKV_WIDTH, 3 * RWKV_WIDTH,
                3 * RWKV_WIDTH + W_LORA, 3 * RWKV_WIDTH + W_LORA + A_LORA]
    rr, kr, vr, wl, al, gl = jnp.split(zs, split_at, axis=-1)
    w_log = -jax.nn.softplus(-(w0 + jnp.tanh(wl) @ w_up)) - 0.5
    decay = jnp.exp(-jnp.exp(w_log))
    a = jax.nn.sigmoid(a0 + al @ a_up)
    gate = jax.nn.sigmoid(gl) @ g_up
    kkf = heads(kr * k_k).astype(jnp.float32)
    kk = (kkf / jnp.maximum(jnp.linalg.norm(kkf, axis=-1, keepdims=True), 1e-12)).astype(kr.dtype)
    kr = kr * (1.0 + (a - 1.0) * k_a)
    r_h, k_h, v_h = heads(rr), heads(kr), heads(vr)
    y = rwkv7_scan(r_h, heads(decay), k_h, v_h, -kk, kk * heads(a))
    y = head_group_norm(y, RWKV_GN_EPS, ln_g, ln_b)
    bonus = jnp.sum(r_h * k_h * r_k, axis=-1, keepdims=True) * v_h
    rwkv_out = (y + bonus).reshape(B, T, RWKV_WIDTH) * gate

    return jnp.concatenate([ret_out, rwkv_out], axis=-1) @ w_out


def swa_sink_mixer(h, w_qkv, b_qkv, sinks, w_o, b_o):
    B, T, _ = h.shape
    W = SWA_WINDOW
    NB = T // W
    z = h @ w_qkv + b_qkv
    q, k, v = jnp.split(z, [SWA_WIDTH, SWA_WIDTH + SWA_KV_HEADS * HEAD_DIM], axis=-1)
    q = q.reshape(B, NB, W, SWA_KV_HEADS, SWA_GROUP, HEAD_DIM)
    k = k.reshape(B, NB, W, SWA_KV_HEADS, HEAD_DIM)
    v = v.reshape(B, NB, W, SWA_KV_HEADS, HEAD_DIM)

    def with_prev(t):
        prev = jnp.pad(t, ((0, 0), (1, 0), (0, 0), (0, 0), (0, 0)))[:, :-1]
        return jnp.concatenate([prev, t], axis=2)

    kb, vb = with_prev(k), with_prev(v)
    s = jnp.einsum('bnqhgd,bnkhd->bnhgqk', q, kb).astype(jnp.float32) * HEAD_DIM ** -0.5
    qi = jnp.arange(W)[:, None]
    kj = jnp.arange(2 * W)[None, :]
    rel = qi + W - kj
    band = (rel >= 0) & (rel < W)
    valid = band[None] & ((jnp.arange(NB)[:, None, None] > 0) | (kj[None] >= W))
    s = jnp.where(valid[None, :, None, None], s, -jnp.inf)
    sink = sinks.astype(jnp.float32).reshape(SWA_KV_HEADS, SWA_GROUP)[None, None, :, :, None, None]
    m = jnp.maximum(jnp.max(s, axis=-1, keepdims=True), sink)
    e = jnp.exp(s - m)
    p = e / (jnp.sum(e, axis=-1, keepdims=True) + jnp.exp(sink - m))
    o = jnp.einsum('bnhgqk,bnkhd->bnqhgd', p.astype(vb.dtype), vb)
    return o.reshape(B, T, SWA_WIDTH) @ w_o + b_o


def swiglu(h, w_gate, w_up, w_down):
    return (jax.nn.silu(h @ w_gate) * (h @ w_up)) @ w_down


def setup_inputs(seed: int = 0) -> dict:
    key = jax.random.key(seed)
    ks = jax.random.split(key, 25)
    f32 = jnp.float32
    D, H = D_MODEL, RWKV_HEADS

    def nrm(k, shape, scale):
        return jax.random.normal(k, shape, f32) * scale

    return {
        'x': nrm(ks[0], (BATCH, SEQ, D), 1.0),
        'norm1_g': 1.0 + nrm(ks[1], (DEPTH, D), 0.02),
        'norm2_g': 1.0 + nrm(ks[2], (DEPTH, D), 0.02),
        'final_g': 1.0 + nrm(ks[3], (D,), 0.02),
        'even_w_in': nrm(ks[4], (N_EVEN, D, EVEN_IN), D ** -0.5),
        'even_w_out': nrm(ks[5], (N_EVEN, MIX_WIDTH, D), MIX_WIDTH ** -0.5),
        'rwkv_mu': jax.random.uniform(ks[6], (N_EVEN, RWKV_SHIFT_WIDTH), f32),
        'rwkv_w0': nrm(ks[7], (N_EVEN, RWKV_WIDTH), 0.5),
        'rwkv_w_up': nrm(ks[8], (N_EVEN, W_LORA, RWKV_WIDTH), 0.5 * W_LORA ** -0.5),
        'rwkv_a0': nrm(ks[9], (N_EVEN, RWKV_WIDTH), 0.5),
        'rwkv_a_up': nrm(ks[10], (N_EVEN, A_LORA, RWKV_WIDTH), 0.5 * A_LORA ** -0.5),
        'rwkv_g_up': nrm(ks[11], (N_EVEN, G_LORA, RWKV_WIDTH), G_LORA ** -0.5),
        'rwkv_k_k': 0.85 + nrm(ks[12], (N_EVEN, RWKV_WIDTH), 0.05),
        'rwkv_k_a': 1.0 + nrm(ks[13], (N_EVEN, RWKV_WIDTH), 0.05),
        'rwkv_r_k': nrm(ks[14], (N_EVEN, H, HEAD_DIM), 0.1),
        'rwkv_ln_g': 1.0 + nrm(ks[15], (N_EVEN, H, HEAD_DIM), 0.02),
        'rwkv_ln_b': nrm(ks[16], (N_EVEN, H, HEAD_DIM), 0.02),
        'swa_w_qkv': nrm(ks[17], (N_ODD, D, SWA_QKV), D ** -0.5),
        'swa_b_qkv': nrm(ks[18], (N_ODD, SWA_QKV), 0.02),
        'swa_sinks': nrm(ks[19], (N_ODD, SWA_HEADS), 0.5),
        'swa_w_o': nrm(ks[20], (N_ODD, SWA_WIDTH, D), SWA_WIDTH ** -0.5),
        'swa_b_o': nrm(ks[21], (N_ODD, D), 0.02),
        'ffn_w_gate': nrm(ks[22], (DEPTH, D, D_FF), D ** -0.5),
        'ffn_w_up': nrm(ks[23], (DEPTH, D, D_FF), D ** -0.5),
        'ffn_w_down': nrm(ks[24], (DEPTH, D_FF, D), D_FF ** -0.5),
    }


def reference(x, norm1_g, norm2_g, final_g, even_w_in, even_w_out, rwkv_mu, rwkv_w0, rwkv_w_up,
              rwkv_a0, rwkv_a_up, rwkv_g_up, rwkv_k_k, rwkv_k_a, rwkv_r_k, rwkv_ln_g, rwkv_ln_b,
              swa_w_qkv, swa_b_qkv, swa_sinks, swa_w_o, swa_b_o, ffn_w_gate, ffn_w_up, ffn_w_down):
    h = x
    for layer in range(DEPTH):
        i = layer // 2
        n = rms_norm(h, norm1_g[layer])
        if layer % 2 == 0:
            mix = retention_rwkv_mixer(n, even_w_in[i], even_w_out[i], rwkv_mu[i], rwkv_w0[i], rwkv_w_up[i],
                                       rwkv_a0[i], rwkv_a_up[i], rwkv_g_up[i], rwkv_k_k[i], rwkv_k_a[i],
                                       rwkv_r_k[i], rwkv_ln_g[i], rwkv_ln_b[i])
        else:
            mix = swa_sink_mixer(n, swa_w_qkv[i], swa_b_qkv[i], swa_sinks[i], swa_w_o[i], swa_b_o[i])
        h = h + mix
        n = rms_norm(h, norm2_g[layer])
        h = h + swiglu(n, ffn_w_gate[layer], ffn_w_up[layer], ffn_w_down[layer])
    return rms_norm(h, final_g)
```

```python
import functools

import numpy as np
import jax
import jax.numpy as jnp
from jax import lax
from jax.experimental import pallas as pl
from jax.experimental.pallas import tpu as pltpu

F32 = jnp.float32
BF16 = jnp.bfloat16

LANES = 128
HEAD_DIM = 64
PAIR = LANES // HEAD_DIM
RMS_EPS = 1e-6
RET_GN_EPS = 1e-6
RWKV_GN_EPS = 64e-5
ROPE_BASE = 10000.0
RET_CHUNK = 128
RWKV_CHUNK = 64
SWA_WINDOW = 128
VMEM_LIMIT = 56 * 1024 * 1024


def _dot(a, b):
    return jnp.dot(a.astype(BF16), b.astype(BF16), preferred_element_type=F32)


def _dot_nt(a, b):
    return lax.dot_general(a.astype(BF16), b.astype(BF16), (((1,), (1,)), ((), ())),
                           preferred_element_type=F32)


def _split2(x):
    hi = x.astype(BF16)
    lo = (x - hi.astype(F32)).astype(BF16)
    return hi, lo


def _dot_hi2(x, m):
    hi, lo = _split2(x)
    return (jnp.dot(hi, m, preferred_element_type=F32)
            + jnp.dot(lo, m, preferred_element_type=F32))


def _dot_hi3_left(m, x):
    hi = x.astype(BF16)
    r1 = x - hi.astype(F32)
    mid = r1.astype(BF16)
    lo = (r1 - mid.astype(F32)).astype(BF16)
    return (jnp.dot(m, hi, preferred_element_type=F32)
            + jnp.dot(m, mid, preferred_element_type=F32)
            + jnp.dot(m, lo, preferred_element_type=F32))


def _iota2(shape, dim):
    return lax.broadcasted_iota(jnp.int32, shape, dim)


def _head_mask(rows):
    return _iota2((rows, LANES), 1) < HEAD_DIM


def _stack_heads(x, m0):
    zero = jnp.zeros_like(x)
    return jnp.concatenate([jnp.where(m0, x, zero), jnp.where(m0, zero, x)], axis=0)


def _unstack_heads(xs, m0):
    r = xs.shape[0] // 2
    return jnp.where(m0, xs[:r], xs[r:])


def _group_matrix(scale):
    r = _iota2((LANES, LANES), 0) // HEAD_DIM
    c = _iota2((LANES, LANES), 1) // HEAD_DIM
    return jnp.where(r == c, scale, 0.0).astype(BF16)


def _rms_norm(x, g):
    ms = jnp.mean(x * x, axis=-1, keepdims=True)
    return x * lax.rsqrt(ms + RMS_EPS) * g


def _norm_proj_kernel(*refs, has_bias):
    if has_bias:
        h_ref, g_ref, w_ref, b_ref, o_ref = refs
    else:
        h_ref, g_ref, w_ref, o_ref = refs
    n = _rms_norm(h_ref[...], g_ref[...])
    z = jnp.dot(n.astype(BF16), w_ref[...], preferred_element_type=F32)
    if has_bias:
        z = z + b_ref[...]
    o_ref[...] = z


def _const_spec(shape):
    nd = len(shape)
    return pl.BlockSpec(shape, lambda *_: (0,) * nd, pipeline_mode=pl.Buffered(1))


def _norm_proj(h2, g, w, bias, tm=256):
    m, d = h2.shape
    n = w.shape[1]
    in_specs = [pl.BlockSpec((tm, d), lambda i: (i, 0)), _const_spec((1, d)), _const_spec((d, n))]
    args = [h2, g.reshape(1, d), w.astype(BF16)]
    if bias is not None:
        in_specs.append(_const_spec((1, n)))
        args.append(bias.reshape(1, n))
    return pl.pallas_call(
        functools.partial(_norm_proj_kernel, has_bias=bias is not None),
        grid=(m // tm,),
        in_specs=in_specs,
        out_specs=pl.BlockSpec((tm, n), lambda i: (i, 0)),
        out_shape=jax.ShapeDtypeStruct((m, n), F32),
        compiler_params=pltpu.CompilerParams(dimension_semantics=("parallel",),
                                             vmem_limit_bytes=VMEM_LIMIT),
        name="norm_proj",
    )(*args)


def _retention_kernel(zq_ref, zqr_ref, zk_ref, zkr_ref, zv_ref, zg_ref, cos_ref, sin_ref,
                      dmask_ref, qdec_ref, kdec_ref, cdec_ref, o_ref, s_ref, *, n_chunks):
    @pl.when(pl.program_id(2) == 0)
    def _():
        s_ref[...] = jnp.zeros_like(s_ref)

    c = RET_CHUNK
    m0 = _head_mask(c)
    gmean = _group_matrix(1.0 / HEAD_DIM)
    r = _iota2((LANES, LANES), 0) // HEAD_DIM
    cc = _iota2((LANES, LANES), 1) // HEAD_DIM
    same_head = r == cc
    dmask = dmask_ref[0]
    qdec = qdec_ref[0]
    kdec = kdec_ref[0]
    cdec = cdec_ref[0]

    def body(j, carry):
        rows = pl.ds(pl.multiple_of(j * c, c), c)
        cos = cos_ref[rows, :]
        sin = sin_ref[rows, :]
        q = zq_ref[rows, :] * cos + zqr_ref[rows, :] * sin
        k = (zk_ref[rows, :] * cos + zkr_ref[rows, :] * sin) * (HEAD_DIM ** -0.5)
        v = zv_ref[rows, :]
        g = zg_ref[rows, :]
        state = s_ref[...]
        scores = _dot_nt(_stack_heads(q, m0), k) * dmask
        o_inner = _unstack_heads(_dot(scores, v), m0)
        o_cross = _dot(q * qdec, state)
        kv = _dot((k * kdec).T, v)
        s_ref[...] = cdec * state + jnp.where(same_head, kv, 0.0)
        o = o_inner + o_cross
        mu = _dot_hi2(o, gmean)
        d = o - mu
        var = _dot_hi2(d * d, gmean)
        o_ref[rows, :] = d * lax.rsqrt(var + RET_GN_EPS) * (g * jax.nn.sigmoid(g))
        return carry

    lax.fori_loop(0, n_chunks, body, 0)


def _retention_tables(dtype):
    c = RET_CHUNK
    n_heads = 8
    h = jnp.arange(n_heads, dtype=dtype)
    log_gamma = jnp.log1p(-(2.0 ** (-5.0 - h)))
    idx = jnp.arange(c, dtype=dtype)
    rel = idx[:, None] - idx[None, :]
    inner = jnp.where(rel >= 0, jnp.exp(jnp.maximum(rel, 0.0)[None] * log_gamma[:, None, None]), 0.0)
    dmask = inner.reshape(n_heads // PAIR, PAIR * c, c)
    lanes_lg = jnp.repeat(log_gamma.reshape(n_heads // PAIR, PAIR), HEAD_DIM, axis=1)
    qdec = jnp.exp((idx + 1.0)[None, :, None] * lanes_lg[:, None, :])
    kdec = jnp.exp((c - 1 - idx)[None, :, None] * lanes_lg[:, None, :])
    cdec = jnp.broadcast_to(jnp.exp(c * lanes_lg)[:, None, :], (n_heads // PAIR, LANES, LANES))
    return dmask, qdec, kdec, cdec


def _retention(z, cos, sin, batch, seq, col, tb=512):
    nt = seq // tb
    n_pairs = 4
    dmask, qdec, kdec, cdec = _retention_tables(z.dtype)

    def zspec(name):
        off = col[name]
        return pl.BlockSpec((tb, LANES), lambda b, p, t: (b * nt + t, off + p))

    tab = lambda shape: pl.BlockSpec((1,) + shape, lambda b, p, t: (p, 0, 0))
    rope = pl.BlockSpec((tb, LANES), lambda b, p, t: (t, 0))
    return pl.pallas_call(
        functools.partial(_retention_kernel, n_chunks=tb // RET_CHUNK),
        grid=(batch, n_pairs, nt),
        in_specs=[zspec("q"), zspec("q_rot"), zspec("k"), zspec("k_rot"), zspec("v"), zspec("g"),
                  rope, rope,
                  tab((PAIR * RET_CHUNK, RET_CHUNK)), tab((RET_CHUNK, LANES)),
                  tab((RET_CHUNK, LANES)), tab((LANES, LANES))],
        out_specs=pl.BlockSpec((tb, LANES), lambda b, p, t: (b * nt + t, p)),
        out_shape=jax.ShapeDtypeStruct((batch * seq, n_pairs * LANES), F32),
        scratch_shapes=[pltpu.VMEM((LANES, LANES), F32)],
        compiler_params=pltpu.CompilerParams(
            dimension_semantics=("parallel", "parallel", "arbitrary"),
            vmem_limit_bytes=VMEM_LIMIT),
        name="retention",
    )(z, z, z, z, z, z, cos, sin, dmask, qdec, kdec, cdec)


def _unit_lower_inverse(n_mat):
    r = _iota2((LANES, LANES), 0)
    c = _iota2((LANES, LANES), 1)
    eye = jnp.where(r == c, 1.0, 0.0).astype(F32)
    p = jnp.where(r // 8 == c // 8, n_mat, 0.0)
    t = eye + p
    p = _dot(p, p)
    t = t + _dot(t, p)
    p = _dot(p, p)
    t = t + _dot(t, p)
    for k in (8, 16, 32):
        lower_left = (r // (2 * k) == c // (2 * k)) & (r // k > c // k)
        t = t + _dot(t, _dot(jnp.where(lower_left, n_mat, 0.0), t))
    return t


def _rwkv_kernel(zr_ref, zk_ref, zv_ref, zl_ref, mur_ref, muk_ref, muv_ref, mul_ref,
                 w0_ref, wup_ref, a0_ref, aup_ref, gup_ref, kk_ref, ka_ref, rk_ref, lng_ref, lnb_ref,
                 o_ref,
                 ht_ref, prev_ref, prevl_ref, r_s, lw_s, k_s, v_s, a_s, b_s, y_s, *, tb):
    t_idx = pl.program_id(2)

    @pl.when(t_idx == 0)
    def _():
        ht_ref[...] = jnp.zeros_like(ht_ref)
        prev_ref[...] = jnp.zeros_like(prev_ref)
        prevl_ref[...] = jnp.zeros_like(prevl_ref)

    gsum = _group_matrix(1.0)
    gmean = _group_matrix(1.0 / HEAD_DIM)
    first_row = _iota2((tb, 1), 0) == 0

    def shifted(z, carry_row):
        return jnp.where(first_row, carry_row, pltpu.roll(z, 1, axis=0))

    zr, zk, zv, zl = zr_ref[...], zk_ref[...], zv_ref[...], zl_ref[...]
    xr = zr + mur_ref[...] * (shifted(zr, prev_ref[0:1, :]) - zr)
    xk = zk + muk_ref[...] * (shifted(zk, prev_ref[1:2, :]) - zk)
    xv = zv + muv_ref[...] * (shifted(zv, prev_ref[2:3, :]) - zv)
    xl = zl + mul_ref[...] * (shifted(zl, prevl_ref[0:1, :]) - zl)
    prev_ref[0:1, :] = zr[tb - 1:tb, :]
    prev_ref[1:2, :] = zk[tb - 1:tb, :]
    prev_ref[2:3, :] = zv[tb - 1:tb, :]
    prevl_ref[0:1, :] = zl[tb - 1:tb, :]

    x_wa = xl[:, :LANES]
    x_g = xl[:, LANES:]
    w_log = -jax.nn.softplus(-(w0_ref[...] + _dot(jnp.tanh(x_wa), wup_ref[...]))) - 0.5
    log_decay = -jnp.exp(w_log)
    a = jax.nn.sigmoid(a0_ref[...] + _dot(x_wa, aup_ref[...]))
    gate = _dot(jax.nn.sigmoid(x_g), gup_ref[...])
    kkf = xk * kk_ref[...]
    norm = jnp.sqrt(_dot_hi2(kkf * kkf, gsum))
    kk = kkf / jnp.maximum(norm, 1e-12)
    k2 = xk * (1.0 + (a - 1.0) * ka_ref[...])
    bonus = _dot_hi2(xr * k2 * rk_ref[...], gsum) * xv

    r_s[...] = xr
    lw_s[...] = log_decay
    k_s[...] = k2
    v_s[...] = xv
    a_s[...] = -kk
    b_s[...] = kk * a

    c = RWKV_CHUNK
    m0 = _head_mask(c)
    ri = _iota2((LANES, LANES), 0)
    ci = _iota2((LANES, LANES), 1)
    same_head = ri // c == ci // c
    strict = same_head & (ri > ci)
    incl = same_head & (ri >= ci)
    tri = jnp.where(_iota2((c, c), 0) >= _iota2((c, c), 1), 1.0, 0.0).astype(BF16)

    def body(j, carry):
        rows = pl.ds(pl.multiple_of(j * c, c), c)
        lw = lw_s[rows, :]
        cum = _dot_hi3_left(tri, lw)
        cum_last = cum[c - 1:c, :]
        g_inv = jnp.exp(-cum)
        a_t = a_s[rows, :] * jnp.exp(cum - lw)
        r_t = r_s[rows, :] * jnp.exp(cum)
        b_t = b_s[rows, :] * g_inv
        k_t = k_s[rows, :] * g_inv
        g_rem = jnp.exp(cum_last - cum)
        v = v_s[rows, :]

        a_st = _stack_heads(a_t, m0)
        r_st = _stack_heads(r_t, m0)
        v_st = _stack_heads(v, m0)
        b2 = jnp.concatenate([b_t, b_t], axis=0)
        k2_ = jnp.concatenate([k_t, k_t], axis=0)
        n_ab = jnp.where(strict, _dot_nt(a_st, b2), 0.0)
        n_ak = jnp.where(strict, _dot_nt(a_st, k2_), 0.0)
        n_rb = jnp.where(incl, _dot_nt(r_st, b2), 0.0)
        n_rk = jnp.where(incl, _dot_nt(r_st, k2_), 0.0)

        ht = ht_ref[...]
        t_inv = _unit_lower_inverse(n_ab)
        u = _dot(t_inv, _dot_nt(a_st, ht) + _dot(n_ak, v_st))
        y_st = _dot_nt(r_st, ht) + _dot(n_rb, u) + _dot(n_rk, v_st)
        y_s[rows, :] = y_st[:c] + y_st[c:]

        uv = jnp.concatenate([u, v_st], axis=0)
        bk = jnp.concatenate([_stack_heads(b_s[rows, :] * g_rem, m0),
                              _stack_heads(k_s[rows, :] * g_rem, m0)], axis=0)
        ht_ref[...] = ht * jnp.exp(cum_last) + _dot(uv.T, bk)
        return carry

    lax.fori_loop(0, tb // c, body, 0)

    y = y_s[...]
    mu = _dot_hi2(y, gmean)
    d = y - mu
    var = _dot_hi2(d * d, gmean)
    yn = d * lax.rsqrt(var + RWKV_GN_EPS) * lng_ref[...] + lnb_ref[...]
    o_ref[...] = (yn + bonus) * gate


def _rwkv(z, params, batch, seq, col, tb=512):
    nt = seq // tb
    n_pairs = 4

    def zspec(name, width=LANES):
        off = col[name]
        if width == LANES:
            return pl.BlockSpec((tb, LANES), lambda b, p, t: (b * nt + t, off + p))
        return pl.BlockSpec((tb, width), lambda b, p, t: (b * nt + t, off * LANES // width))

    def pair_row(x):
        return x.reshape(1, -1), pl.BlockSpec((1, LANES), lambda b, p, t: (0, p))

    def pair_cols(x):
        return x.astype(BF16), pl.BlockSpec((x.shape[0], LANES), lambda b, p, t: (0, p))

    def whole_row(x):
        return x.reshape(1, -1), pl.BlockSpec((1, x.size), lambda b, p, t: (0, 0))

    args, specs = [z, z, z, z], [zspec("rr"), zspec("kr"), zspec("vr"), zspec("lora", 2 * LANES)]
    for a, s in (pair_row(params["mu_r"]), pair_row(params["mu_k"]), pair_row(params["mu_v"]),
                 whole_row(params["mu_l"]),
                 pair_row(params["w0"]), pair_cols(params["w_up"]),
                 pair_row(params["a0"]), pair_cols(params["a_up"]), pair_cols(params["g_up"]),
                 pair_row(params["k_k"]), pair_row(params["k_a"]), pair_row(params["r_k"]),
                 pair_row(params["ln_g"]), pair_row(params["ln_b"])):
        args.append(a)
        specs.append(s)

    blk = pltpu.VMEM((tb, LANES), F32)
    return pl.pallas_call(
        functools.partial(_rwkv_kernel, tb=tb),
        grid=(batch, n_pairs, nt),
        in_specs=specs,
        out_specs=pl.BlockSpec((tb, LANES), lambda b, p, t: (b * nt + t, p)),
        out_shape=jax.ShapeDtypeStruct((batch * seq, n_pairs * LANES), F32),
        scratch_shapes=[pltpu.VMEM((LANES, LANES), F32),
                        pltpu.VMEM((8, LANES), F32),
                        pltpu.VMEM((8, 2 * LANES), F32),
                        blk, blk, blk, blk, blk, blk, blk],
        compiler_params=pltpu.CompilerParams(
            dimension_semantics=("parallel", "parallel", "arbitrary"),
            vmem_limit_bytes=VMEM_LIMIT),
        name="rwkv7",
    )(*args)


def _swa_kernel(sink_ref, q_ref, kv_ref, kvp_ref, o_ref, *, n_qblocks, layer_sink_base):
    n = pl.program_id(1)
    w = SWA_WINDOW
    m0 = _head_mask(w)
    qi = _iota2((PAIR * w, 2 * w), 0) % w
    kj = _iota2((PAIR * w, 2 * w), 1)
    rel = qi + w - kj
    first_key = jnp.where(n > 0, 0, w)
    valid = (rel >= 0) & (rel < w) & (kj >= first_key)
    second_head = _iota2((PAIR * w, 1), 0) >= w
    kv = kv_ref[...]
    kvp = kvp_ref[...]
    n_kv_cols = kv.shape[1] // 2
    for qb in range(n_qblocks):
        h = qb // 2
        q = q_ref[:, qb * LANES:(qb + 1) * LANES]
        kd = jnp.concatenate([kvp[:, h * LANES:(h + 1) * LANES], kv[:, h * LANES:(h + 1) * LANES]], axis=0)
        vd = jnp.concatenate([kvp[:, n_kv_cols + h * LANES:n_kv_cols + (h + 1) * LANES],
                              kv[:, n_kv_cols + h * LANES:n_kv_cols + (h + 1) * LANES]], axis=0)
        s = _dot_nt(_stack_heads(q, m0), kd) * (HEAD_DIM ** -0.5)
        s = jnp.where(valid, s, -jnp.inf)
        sink = jnp.where(second_head, sink_ref[layer_sink_base + 2 * qb + 1],
                         sink_ref[layer_sink_base + 2 * qb])
        m = jnp.maximum(jnp.max(s, axis=-1, keepdims=True), sink)
        e = jnp.exp(s - m)
        p = e / (jnp.sum(e, axis=-1, keepdims=True) + jnp.exp(sink - m))
        o_ref[:, qb * LANES:(qb + 1) * LANES] = _unstack_heads(_dot(p, vd), m0)


def _swa(z, sinks_flat, layer, batch, seq, n_heads=16):
    w = SWA_WINDOW
    nb = seq // w
    qw = n_heads * HEAD_DIM
    kvw = z.shape[1] - qw
    assert kvw == qw, "duplicated k|v block must be as wide as the q block"
    return pl.pallas_call(
        functools.partial(_swa_kernel, n_qblocks=qw // LANES, layer_sink_base=layer * n_heads),
        grid=(batch, nb),
        in_specs=[pl.BlockSpec(memory_space=pltpu.SMEM),
                  pl.BlockSpec((w, qw), lambda b, n: (b * nb + n, 0)),
                  pl.BlockSpec((w, kvw), lambda b, n: (b * nb + n, 1)),
                  pl.BlockSpec((w, kvw), lambda b, n: (b * nb + jnp.maximum(n - 1, 0), 1))],
        out_specs=pl.BlockSpec((w, qw), lambda b, n: (b * nb + n, 0)),
        out_shape=jax.ShapeDtypeStruct((batch * seq, qw), F32),
        compiler_params=pltpu.CompilerParams(dimension_semantics=("parallel", "arbitrary"),
                                             vmem_limit_bytes=VMEM_LIMIT),
        name="swa",
    )(sinks_flat, z, z, z)


def _post_kernel(*refs, n_mix, has_bias, has_final):
    it = iter(refs)
    h_ref = next(it)
    mix_refs = [next(it) for _ in range(n_mix)]
    wo_refs = [next(it) for _ in range(n_mix)]
    bo_ref = next(it) if has_bias else None
    g2_ref, wg_ref, wu_ref, wd_ref = next(it), next(it), next(it), next(it)
    gf_ref = next(it) if has_final else None
    o_ref = next(it)

    h = h_ref[...]
    for m_ref, w_ref in zip(mix_refs, wo_refs):
        h = h + jnp.dot(m_ref[...].astype(BF16), w_ref[...], preferred_element_type=F32)
    if has_bias:
        h = h + bo_ref[...]
    n = _rms_norm(h, g2_ref[...]).astype(BF16)
    gate = jnp.dot(n, wg_ref[...], preferred_element_type=F32)
    up = jnp.dot(n, wu_ref[...], preferred_element_type=F32)
    act = (gate * jax.nn.sigmoid(gate) * up).astype(BF16)
    h = h + jnp.dot(act, wd_ref[...], preferred_element_type=F32)
    if has_final:
        h = _rms_norm(h, gf_ref[...])
    o_ref[...] = h


def _post(h2, mixes, wos, bo, g2, wg, wu, wd, final_g, tm=256):
    m, d = h2.shape
    row = lambda width: pl.BlockSpec((tm, width), lambda i: (i, 0))
    args, specs = [h2], [row(d)]
    for mx in mixes:
        args.append(mx)
        specs.append(row(mx.shape[1]))
    for w in wos:
        args.append(w.astype(BF16))
        specs.append(_const_spec(w.shape))
    if bo is not None:
        args.append(bo.reshape(1, d))
        specs.append(_const_spec((1, d)))
    for a in (g2.reshape(1, d), wg.astype(BF16), wu.astype(BF16), wd.astype(BF16)):
        args.append(a)
        specs.append(_const_spec(a.shape))
    if final_g is not None:
        args.append(final_g.reshape(1, d))
        specs.append(_const_spec((1, d)))
    return pl.pallas_call(
        functools.partial(_post_kernel, n_mix=len(mixes), has_bias=bo is not None,
                          has_final=final_g is not None),
        grid=(m // tm,),
        in_specs=specs,
        out_specs=row(d),
        out_shape=jax.ShapeDtypeStruct((m, d), F32),
        compiler_params=pltpu.CompilerParams(dimension_semantics=("parallel",),
                                             vmem_limit_bytes=VMEM_LIMIT),
        name="post_ffn",
    )(*args)


def _rotate_half_columns(w):
    d, n = w.shape
    w4 = w.reshape(d, n // HEAD_DIM, 2, HEAD_DIM // 2)
    return jnp.concatenate([-w4[:, :, 1:2], w4[:, :, 0:1]], axis=2).reshape(d, n)


def _even_layout(w_in, mu, w_up, a_up):
    rw = 512
    q, k, v, g = (w_in[:, i * rw:(i + 1) * rw] for i in range(4))
    rest = w_in[:, 4 * rw:]
    w = jnp.concatenate([q, _rotate_half_columns(q), k, _rotate_half_columns(k), v, g, rest], axis=1)
    names = ["q", "q_rot", "k", "k_rot", "v", "g", "rr", "kr", "vr", "lora"]
    col = {nm: i * (rw // LANES) for i, nm in enumerate(names)}
    lora = 64
    zeros = jnp.zeros((lora, w_up.shape[1]), w_up.dtype)
    params = {
        "mu_r": mu[:rw], "mu_k": mu[rw:2 * rw], "mu_v": mu[2 * rw:3 * rw], "mu_l": mu[3 * rw:],
        "w_up": jnp.concatenate([w_up, zeros], axis=0),
        "a_up": jnp.concatenate([zeros, a_up], axis=0),
    }
    return w, col, params


def _swa_layout(w_qkv, b_qkv, n_heads=16, n_kv=4):
    qw = n_heads * HEAD_DIM
    kw = n_kv * HEAD_DIM

    def dup(x):
        lead = x.shape[:-1]
        x4 = x.reshape(lead + (n_kv, 1, HEAD_DIM))
        return jnp.broadcast_to(x4, lead + (n_kv, PAIR, HEAD_DIM)).reshape(lead + (n_kv * LANES,))

    w = jnp.concatenate([w_qkv[:, :qw], dup(w_qkv[:, qw:qw + kw]), dup(w_qkv[:, qw + kw:])], axis=1)
    b = jnp.concatenate([b_qkv[:qw], dup(b_qkv[qw:qw + kw]), dup(b_qkv[qw + kw:])], axis=0)
    return w, b


def _rope_tables(seq, dtype):
    half = HEAD_DIM // 2
    inv_freq = ROPE_BASE ** (-jnp.linspace(0.0, 1.0, half, dtype=dtype))
    ang = jnp.arange(seq, dtype=dtype)[:, None] * inv_freq[None, :]
    reps = LANES // half
    return jnp.tile(jnp.cos(ang), (1, reps)), jnp.tile(jnp.sin(ang), (1, reps))


def kernel(x, norm1_g, norm2_g, final_g, even_w_in, even_w_out, rwkv_mu, rwkv_w0, rwkv_w_up,
           rwkv_a0, rwkv_a_up, rwkv_g_up, rwkv_k_k, rwkv_k_a, rwkv_r_k, rwkv_ln_g, rwkv_ln_b,
           swa_w_qkv, swa_b_qkv, swa_sinks, swa_w_o, swa_b_o, ffn_w_gate, ffn_w_up, ffn_w_down):
    batch, seq, d = x.shape
    depth = norm1_g.shape[0]
    h = x.reshape(batch * seq, d)
    cos, sin = _rope_tables(seq, x.dtype)
    sinks_flat = swa_sinks.reshape(-1)
    ret_w = 512
    for layer in range(depth):
        i = layer // 2
        last = layer == depth - 1
        if layer % 2 == 0:
            w, col, params = _even_layout(even_w_in[i], rwkv_mu[i], rwkv_w_up[i], rwkv_a_up[i])
            params.update(w0=rwkv_w0[i], a0=rwkv_a0[i], g_up=rwkv_g_up[i], k_k=rwkv_k_k[i],
                          k_a=rwkv_k_a[i], r_k=rwkv_r_k[i].reshape(-1),
                          ln_g=rwkv_ln_g[i].reshape(-1), ln_b=rwkv_ln_b[i].reshape(-1))
            z = _norm_proj(h, norm1_g[layer], w, None)
            ret_out = _retention(z, cos, sin, batch, seq, col)
            rwkv_out = _rwkv(z, params, batch, seq, col)
            mixes = [ret_out, rwkv_out]
            wos = [even_w_out[i][:ret_w], even_w_out[i][ret_w:]]
            bo = None
        else:
            w, b = _swa_layout(swa_w_qkv[i], swa_b_qkv[i])
            z = _norm_proj(h, norm1_g[layer], w, b)
            mixes = [_swa(z, sinks_flat, i, batch, seq)]
            wos = [swa_w_o[i]]
            bo = swa_b_o[i]
        h = _post(h, mixes, wos, bo, norm2_g[layer], ffn_w_gate[layer], ffn_w_up[layer],
                  ffn_w_down[layer], final_g if last else None)
    return h.reshape(batch, seq, d)
```

```python
import functools

import numpy as np
import jax
import jax.numpy as jnp
from jax import lax
from jax.experimental import pallas as pl
from jax.experimental.pallas import tpu as pltpu

F32 = jnp.float32
BF16 = jnp.bfloat16

LANES = 128
HEAD_DIM = 64
PAIR = LANES // HEAD_DIM
RMS_EPS = 1e-6
RET_GN_EPS = 1e-6
RWKV_GN_EPS = 64e-5
ROPE_BASE = 10000.0
RET_CHUNK = 128
RWKV_CHUNK = 64
SWA_WINDOW = 128
VMEM_LIMIT = 56 * 1024 * 1024


def _dot(a, b):
    return jnp.dot(a.astype(BF16), b.astype(BF16), preferred_element_type=F32)


def _dot_nt(a, b):
    return lax.dot_general(a.astype(BF16), b.astype(BF16), (((1,), (1,)), ((), ())),
                           preferred_element_type=F32)


def _split2(x):
    hi = x.astype(BF16)
    lo = (x - hi.astype(F32)).astype(BF16)
    return hi, lo


def _dot_hi2(x, m):
    hi, lo = _split2(x)
    return (jnp.dot(hi, m, preferred_element_type=F32)
            + jnp.dot(lo, m, preferred_element_type=F32))


def _dot_hi3_left(m, x):
    hi = x.astype(BF16)
    r1 = x - hi.astype(F32)
    mid = r1.astype(BF16)
    lo = (r1 - mid.astype(F32)).astype(BF16)
    return (jnp.dot(m, hi, preferred_element_type=F32)
            + jnp.dot(m, mid, preferred_element_type=F32)
            + jnp.dot(m, lo, preferred_element_type=F32))


def _iota2(shape, dim):
    return lax.broadcasted_iota(jnp.int32, shape, dim)


def _head_mask(rows):
    return _iota2((rows, LANES), 1) < HEAD_DIM


def _stack_heads(x, m0):
    zero = jnp.zeros_like(x)
    return jnp.concatenate([jnp.where(m0, x, zero), jnp.where(m0, zero, x)], axis=0)


def _unstack_heads(xs, m0):
    r = xs.shape[0] // 2
    return jnp.where(m0, xs[:r], xs[r:])


def _group_matrix(scale):
    r = _iota2((LANES, LANES), 0) // HEAD_DIM
    c = _iota2((LANES, LANES), 1) // HEAD_DIM
    return jnp.where(r == c, scale, 0.0).astype(BF16)


def _rms_norm(x, g):
    ms = jnp.mean(x * x, axis=-1, keepdims=True)
    return x * lax.rsqrt(ms + RMS_EPS) * g


def _norm_proj_kernel(*refs, has_bias):
    if has_bias:
        h_ref, g_ref, w_ref, b_ref, o_ref = refs
    else:
        h_ref, g_ref, w_ref, o_ref = refs
    n = _rms_norm(h_ref[...], g_ref[...])
    z = jnp.dot(n.astype(BF16), w_ref[...], preferred_element_type=F32)
    if has_bias:
        z = z + b_ref[...]
    o_ref[...] = z.astype(o_ref.dtype)


def _const_spec(shape):
    nd = len(shape)
    return pl.BlockSpec(shape, lambda *_: (0,) * nd, pipeline_mode=pl.Buffered(1))


def _norm_proj(h2, g, w, bias, out_dtype, tm=256):
    m, d = h2.shape
    n = w.shape[1]
    in_specs = [pl.BlockSpec((tm, d), lambda i: (i, 0)), _const_spec((1, d)), _const_spec((d, n))]
    args = [h2, g.reshape(1, d), w.astype(BF16)]
    if bias is not None:
        in_specs.append(_const_spec((1, n)))
        args.append(bias.reshape(1, n))
    return pl.pallas_call(
        functools.partial(_norm_proj_kernel, has_bias=bias is not None),
        grid=(m // tm,),
        in_specs=in_specs,
        out_specs=pl.BlockSpec((tm, n), lambda i: (i, 0)),
        out_shape=jax.ShapeDtypeStruct((m, n), out_dtype),
        compiler_params=pltpu.CompilerParams(dimension_semantics=("parallel",),
                                             vmem_limit_bytes=VMEM_LIMIT),
        name="norm_proj",
    )(*args)


def _retention_kernel(zq_ref, zqr_ref, zk_ref, zkr_ref, zv_ref, zg_ref, cos_ref, sin_ref,
                      dmask_ref, qdec_ref, kdec_ref, cdec_ref, o_ref, s_ref, *, n_chunks):
    @pl.when(pl.program_id(2) == 0)
    def _():
        s_ref[...] = jnp.zeros_like(s_ref)

    c = RET_CHUNK
    m0 = _head_mask(c)
    gmean = _group_matrix(1.0 / HEAD_DIM)
    r = _iota2((LANES, LANES), 0) // HEAD_DIM
    cc = _iota2((LANES, LANES), 1) // HEAD_DIM
    same_head = r == cc
    dmask = dmask_ref[0]
    qdec = qdec_ref[0]
    kdec = kdec_ref[0]
    cdec = cdec_ref[0]

    def inner_terms(j):
        rows = slice(j * c, (j + 1) * c)
        cos = cos_ref[rows, :]
        sin = sin_ref[rows, :]
        q = zq_ref[rows, :] * cos + zqr_ref[rows, :] * sin
        k = (zk_ref[rows, :] * cos + zkr_ref[rows, :] * sin) * (HEAD_DIM ** -0.5)
        v = zv_ref[rows, :].astype(BF16)
        scores = _dot_nt(_stack_heads(q, m0), k) * dmask
        kv = jnp.where(same_head, _dot((k * kdec).T, v), 0.0)
        yield
        o_inner = _unstack_heads(_dot(scores, v), m0)
        return (q * qdec).astype(BF16), o_inner, kv

    inner = _lockstep(inner_terms(j) for j in range(n_chunks))

    state = s_ref[...]
    states = []
    for _, _, kv in inner:
        states.append(state)
        state = cdec * state + kv
    s_ref[...] = state

    def finish(j):
        rows = slice(j * c, (j + 1) * c)
        q_dec, o_inner, _ = inner[j]
        o = o_inner + _dot(q_dec, states[j])
        yield
        mu = _dot_hi2(o, gmean)
        yield
        d = o - mu
        var = _dot_hi2(d * d, gmean)
        yield
        g = zg_ref[rows, :]
        out = d * lax.rsqrt(var + RET_GN_EPS) * (g * jax.nn.sigmoid(g))
        o_ref[rows, :] = out.astype(o_ref.dtype)

    _lockstep(finish(j) for j in range(n_chunks))


def _retention_tables(dtype):
    c = RET_CHUNK
    n_heads = 8
    h = jnp.arange(n_heads, dtype=dtype)
    log_gamma = jnp.log1p(-(2.0 ** (-5.0 - h)))
    idx = jnp.arange(c, dtype=dtype)
    rel = idx[:, None] - idx[None, :]
    inner = jnp.where(rel >= 0, jnp.exp(jnp.maximum(rel, 0.0)[None] * log_gamma[:, None, None]), 0.0)
    dmask = inner.reshape(n_heads // PAIR, PAIR * c, c)
    lanes_lg = jnp.repeat(log_gamma.reshape(n_heads // PAIR, PAIR), HEAD_DIM, axis=1)
    qdec = jnp.exp((idx + 1.0)[None, :, None] * lanes_lg[:, None, :])
    kdec = jnp.exp((c - 1 - idx)[None, :, None] * lanes_lg[:, None, :])
    cdec = jnp.broadcast_to(jnp.exp(c * lanes_lg)[:, None, :], (n_heads // PAIR, LANES, LANES))
    return dmask, qdec, kdec, cdec


def _retention(z, cos, sin, batch, seq, col, tb=1024):
    nt = seq // tb
    n_pairs = 4
    dmask, qdec, kdec, cdec = _retention_tables(z.dtype)

    def zspec(name):
        off = col[name]
        return pl.BlockSpec((tb, LANES), lambda b, p, t: (b * nt + t, off + p))

    tab = lambda shape: pl.BlockSpec((1,) + shape, lambda b, p, t: (p, 0, 0))
    rope = pl.BlockSpec((tb, LANES), lambda b, p, t: (t, 0))
    return pl.pallas_call(
        functools.partial(_retention_kernel, n_chunks=tb // RET_CHUNK),
        grid=(batch, n_pairs, nt),
        in_specs=[zspec("q"), zspec("q_rot"), zspec("k"), zspec("k_rot"), zspec("v"), zspec("g"),
                  rope, rope,
                  tab((PAIR * RET_CHUNK, RET_CHUNK)), tab((RET_CHUNK, LANES)),
                  tab((RET_CHUNK, LANES)), tab((LANES, LANES))],
        out_specs=pl.BlockSpec((tb, LANES), lambda b, p, t: (b * nt + t, p)),
        out_shape=jax.ShapeDtypeStruct((batch * seq, n_pairs * LANES), BF16),
        scratch_shapes=[pltpu.VMEM((LANES, LANES), F32)],
        compiler_params=pltpu.CompilerParams(
            dimension_semantics=("parallel", "parallel", "arbitrary"),
            vmem_limit_bytes=VMEM_LIMIT),
        name="retention",
    )(z, z, z, z, z, z, cos, sin, dmask, qdec, kdec, cdec)


def _lockstep(gens):
    gens = list(gens)
    results = [None] * len(gens)
    live = list(range(len(gens)))
    while live:
        still = []
        for i in live:
            try:
                next(gens[i])
                still.append(i)
            except StopIteration as stop:
                results[i] = stop.value
        live = still
    return results


def _inverse_masks():
    r = _iota2((LANES, LANES), 0)
    c = _iota2((LANES, LANES), 1)
    eye = jnp.where(r == c, 1.0, 0.0).astype(F32)
    diag8 = r // 8 == c // 8
    lower_left = [(r // (2 * k) == c // (2 * k)) & (r // k > c // k) for k in (8, 16, 32)]
    return eye, diag8, lower_left


def _unit_lower_inverse(n_mat, masks):
    eye, diag8, lower_left = masks
    p = jnp.where(diag8, n_mat, 0.0)
    t = eye + p
    p = _dot(p, p)
    yield
    t = t + _dot(t, p)
    p = _dot(p, p)
    yield
    t = t + _dot(t, p)
    yield
    for mask in lower_left:
        lt = _dot(jnp.where(mask, n_mat, 0.0), t)
        yield
        t = t + _dot(t, lt)
        yield
    return t


def _rwkv_kernel(zr_ref, zk_ref, zv_ref, zl_ref, mur_ref, muk_ref, muv_ref, mul_ref,
                 w0_ref, wup_ref, a0_ref, aup_ref, gup_ref, kk_ref, ka_ref, rk_ref, lng_ref, lnb_ref,
                 o_ref,
                 ht_ref, prev_ref, prevl_ref, r_s, lw_s, k_s, v_s, a_s, b_s, y_s, *, tb):
    t_idx = pl.program_id(2)

    @pl.when(t_idx == 0)
    def _():
        ht_ref[...] = jnp.zeros_like(ht_ref)
        prev_ref[...] = jnp.zeros_like(prev_ref)
        prevl_ref[...] = jnp.zeros_like(prevl_ref)

    gsum = _group_matrix(1.0)
    gmean = _group_matrix(1.0 / HEAD_DIM)
    first_row = _iota2((tb, 1), 0) == 0

    def shifted(z, carry_row):
        return jnp.where(first_row, carry_row, pltpu.roll(z, 1, axis=0))

    zr, zk, zv, zl = zr_ref[...], zk_ref[...], zv_ref[...], zl_ref[...]
    xr = zr + mur_ref[...] * (shifted(zr, prev_ref[0:1, :]) - zr)
    xk = zk + muk_ref[...] * (shifted(zk, prev_ref[1:2, :]) - zk)
    xv = zv + muv_ref[...] * (shifted(zv, prev_ref[2:3, :]) - zv)
    xl = zl + mul_ref[...] * (shifted(zl, prevl_ref[0:1, :]) - zl)
    prev_ref[0:1, :] = zr[tb - 1:tb, :]
    prev_ref[1:2, :] = zk[tb - 1:tb, :]
    prev_ref[2:3, :] = zv[tb - 1:tb, :]
    prevl_ref[0:1, :] = zl[tb - 1:tb, :]

    x_wa = xl[:, :LANES]
    x_g = xl[:, LANES:]
    w_log = -jax.nn.softplus(-(w0_ref[...] + _dot(jnp.tanh(x_wa), wup_ref[...]))) - 0.5
    log_decay = -jnp.exp(w_log)
    a = jax.nn.sigmoid(a0_ref[...] + _dot(x_wa, aup_ref[...]))
    gate = _dot(jax.nn.sigmoid(x_g), gup_ref[...])
    kkf = xk * kk_ref[...]
    norm = jnp.sqrt(_dot_hi2(kkf * kkf, gsum))
    kk = kkf / jnp.maximum(norm, 1e-12)
    k2 = xk * (1.0 + (a - 1.0) * ka_ref[...])
    bonus = _dot_hi2(xr * k2 * rk_ref[...], gsum) * xv

    r_s[...] = xr
    lw_s[...] = log_decay
    k_s[...] = k2
    v_s[...] = xv
    a_s[...] = -kk
    b_s[...] = kk * a

    c = RWKV_CHUNK
    m0 = _head_mask(c)
    ri = _iota2((LANES, LANES), 0)
    ci = _iota2((LANES, LANES), 1)
    same_head = ri // c == ci // c
    strict = same_head & (ri > ci)
    incl = same_head & (ri >= ci)
    tri = jnp.where(_iota2((c, c), 0) >= _iota2((c, c), 1), 1.0, 0.0).astype(BF16)

    inv_masks = _inverse_masks()

    def chunk_terms(j):
        rows = slice(j * c, (j + 1) * c)
        lw = lw_s[rows, :]
        cum = _dot_hi3_left(tri, lw)
        yield
        cum_last = cum[c - 1:c, :]
        g_inv = jnp.exp(-cum)
        g_rem = jnp.exp(cum_last - cum)
        a_st = _stack_heads(a_s[rows, :] * jnp.exp(cum - lw), m0)
        r_st = _stack_heads(r_s[rows, :] * jnp.exp(cum), m0)
        v_st = _stack_heads(v_s[rows, :], m0)
        bg_st = _stack_heads(b_s[rows, :] * g_rem, m0)
        kg_st = _stack_heads(k_s[rows, :] * g_rem, m0)
        b_t = (b_s[rows, :] * g_inv).astype(BF16)
        k_t = (k_s[rows, :] * g_inv).astype(BF16)
        b2 = jnp.concatenate([b_t, b_t], axis=0)
        k2_ = jnp.concatenate([k_t, k_t], axis=0)
        a_bf = a_st.astype(BF16)
        r_bf = r_st.astype(BF16)
        v_bf = v_st.astype(BF16)
        n_ab = jnp.where(strict, _dot_nt(a_bf, b2), 0.0)
        n_ak = jnp.where(strict, _dot_nt(a_bf, k2_), 0.0)
        n_rb = jnp.where(incl, _dot_nt(r_bf, b2), 0.0)
        n_rk = jnp.where(incl, _dot_nt(r_bf, k2_), 0.0)
        yield
        akv = _dot(n_ak, v_bf)
        rkv = _dot(n_rk, v_bf)
        vk = _dot(v_st.T, kg_st)
        t_inv = yield from _unit_lower_inverse(n_ab, inv_masks)
        wu = _dot(t_inv, jnp.concatenate([a_st, akv], axis=1))
        yield
        rw = _dot(n_rb, wu)
        mg = _dot(wu.T, bg_st)
        yield
        return (r_st + rw[:, :LANES], rw[:, LANES:] + rkv, mg[:LANES], mg[LANES:] + vk,
                jnp.exp(cum_last))

    terms = _lockstep(chunk_terms(j) for j in range(tb // c))
    ht = ht_ref[...]
    for j, (r_hat, y_hat, m_mat, g_mat, g_tot) in enumerate(terms):
        y_st = _dot_nt(r_hat, ht) + y_hat
        y_s[j * c:(j + 1) * c, :] = y_st[:c] + y_st[c:]
        ht = ht * g_tot + _dot(ht, m_mat) + g_mat
    ht_ref[...] = ht

    y = y_s[...]
    mu = _dot_hi2(y, gmean)
    d = y - mu
    var = _dot_hi2(d * d, gmean)
    yn = d * lax.rsqrt(var + RWKV_GN_EPS) * lng_ref[...] + lnb_ref[...]
    o_ref[...] = ((yn + bonus) * gate).astype(o_ref.dtype)


def _rwkv(z, params, batch, seq, col, tb=512):
    nt = seq // tb
    n_pairs = 4

    def zspec(name, width=LANES):
        off = col[name]
        if width == LANES:
            return pl.BlockSpec((tb, LANES), lambda b, p, t: (b * nt + t, off + p))
        return pl.BlockSpec((tb, width), lambda b, p, t: (b * nt + t, off * LANES // width))

    def pair_row(x):
        return x.reshape(1, -1), pl.BlockSpec((1, LANES), lambda b, p, t: (0, p))

    def pair_cols(x):
        return x.astype(BF16), pl.BlockSpec((x.shape[0], LANES), lambda b, p, t: (0, p))

    def whole_row(x):
        return x.reshape(1, -1), pl.BlockSpec((1, x.size), lambda b, p, t: (0, 0))

    args, specs = [z, z, z, z], [zspec("rr"), zspec("kr"), zspec("vr"), zspec("lora", 2 * LANES)]
    for a, s in (pair_row(params["mu_r"]), pair_row(params["mu_k"]), pair_row(params["mu_v"]),
                 whole_row(params["mu_l"]),
                 pair_row(params["w0"]), pair_cols(params["w_up"]),
                 pair_row(params["a0"]), pair_cols(params["a_up"]), pair_cols(params["g_up"]),
                 pair_row(params["k_k"]), pair_row(params["k_a"]), pair_row(params["r_k"]),
                 pair_row(params["ln_g"]), pair_row(params["ln_b"])):
        args.append(a)
        specs.append(s)

    blk = pltpu.VMEM((tb, LANES), F32)
    return pl.pallas_call(
        functools.partial(_rwkv_kernel, tb=tb),
        grid=(batch, n_pairs, nt),
        in_specs=specs,
        out_specs=pl.BlockSpec((tb, LANES), lambda b, p, t: (b * nt + t, p)),
        out_shape=jax.ShapeDtypeStruct((batch * seq, n_pairs * LANES), BF16),
        scratch_shapes=[pltpu.VMEM((LANES, LANES), F32),
                        pltpu.VMEM((8, LANES), F32),
                        pltpu.VMEM((8, 2 * LANES), F32),
                        blk, blk, blk, blk, blk, blk, blk],
        compiler_params=pltpu.CompilerParams(
            dimension_semantics=("parallel", "parallel", "arbitrary"),
            vmem_limit_bytes=VMEM_LIMIT),
        name="rwkv7",
    )(*args)


def _swa_kernel(sink_ref, q_ref, kv_ref, kvp_ref, o_ref, *, n_qblocks, layer_sink_base):
    n = pl.program_id(1)
    w = SWA_WINDOW
    m0 = _head_mask(w)
    qi = _iota2((PAIR * w, 2 * w), 0) % w
    kj = _iota2((PAIR * w, 2 * w), 1)
    rel = qi + w - kj
    first_key = jnp.where(n > 0, 0, w)
    valid = (rel >= 0) & (rel < w) & (kj >= first_key)
    second_head = _iota2((PAIR * w, 1), 0) >= w
    kv = kv_ref[...]
    kvp = kvp_ref[...]
    n_kv_cols = kv.shape[1] // 2

    def qblock(qb):
        h = qb // 2
        cols = slice(qb * LANES, (qb + 1) * LANES)
        kcols = slice(h * LANES, (h + 1) * LANES)
        vcols = slice(n_kv_cols + h * LANES, n_kv_cols + (h + 1) * LANES)
        q = q_ref[:, cols] * (HEAD_DIM ** -0.5)
        kd = jnp.concatenate([kvp[:, kcols], kv[:, kcols]], axis=0)
        vd = jnp.concatenate([kvp[:, vcols], kv[:, vcols]], axis=0)
        s = _dot_nt(_stack_heads(q, m0), kd)
        yield
        s = jnp.where(valid, s, -jnp.inf)
        sink = jnp.where(second_head, sink_ref[layer_sink_base + 2 * qb + 1],
                         sink_ref[layer_sink_base + 2 * qb])
        m = jnp.maximum(jnp.max(s, axis=-1, keepdims=True), sink)
        e = jnp.exp(s - m)
        denom = jnp.sum(e, axis=-1, keepdims=True) + jnp.exp(sink - m)
        pv = _dot(e, vd)
        yield
        o_ref[:, cols] = _unstack_heads(pv / denom, m0).astype(o_ref.dtype)

    _lockstep(qblock(qb) for qb in range(n_qblocks))


def _swa(z, sinks_flat, layer, batch, seq, n_heads=16):
    w = SWA_WINDOW
    nb = seq // w
    qw = n_heads * HEAD_DIM
    kvw = z.shape[1] - qw
    assert kvw == qw, "duplicated k|v block must be as wide as the q block"
    return pl.pallas_call(
        functools.partial(_swa_kernel, n_qblocks=qw // LANES, layer_sink_base=layer * n_heads),
        grid=(batch, nb),
        in_specs=[pl.BlockSpec(memory_space=pltpu.SMEM),
                  pl.BlockSpec((w, qw), lambda b, n: (b * nb + n, 0)),
                  pl.BlockSpec((w, kvw), lambda b, n: (b * nb + n, 1)),
                  pl.BlockSpec((w, kvw), lambda b, n: (b * nb + jnp.maximum(n - 1, 0), 1))],
        out_specs=pl.BlockSpec((w, qw), lambda b, n: (b * nb + n, 0)),
        out_shape=jax.ShapeDtypeStruct((batch * seq, qw), BF16),
        compiler_params=pltpu.CompilerParams(dimension_semantics=("parallel", "arbitrary"),
                                             vmem_limit_bytes=VMEM_LIMIT),
        name="swa",
    )(sinks_flat, z, z, z)


def _post_kernel(*refs, n_mix, has_bias, has_final):
    it = iter(refs)
    h_ref = next(it)
    mix_refs = [next(it) for _ in range(n_mix)]
    wo_refs = [next(it) for _ in range(n_mix)]
    bo_ref = next(it) if has_bias else None
    g2_ref, wg_ref, wu_ref, wd_ref = next(it), next(it), next(it), next(it)
    gf_ref = next(it) if has_final else None
    o_ref = next(it)

    h = h_ref[...]
    for m_ref, w_ref in zip(mix_refs, wo_refs):
        h = h + jnp.dot(m_ref[...].astype(BF16), w_ref[...], preferred_element_type=F32)
    if has_bias:
        h = h + bo_ref[...]
    n = _rms_norm(h, g2_ref[...]).astype(BF16)
    gate = jnp.dot(n, wg_ref[...], preferred_element_type=F32)
    up = jnp.dot(n, wu_ref[...], preferred_element_type=F32)
    act = (gate * jax.nn.sigmoid(gate) * up).astype(BF16)
    h = h + jnp.dot(act, wd_ref[...], preferred_element_type=F32)
    if has_final:
        h = _rms_norm(h, gf_ref[...])
    o_ref[...] = h


def _post(h2, mixes, wos, bo, g2, wg, wu, wd, final_g, tm=256):
    m, d = h2.shape
    row = lambda width: pl.BlockSpec((tm, width), lambda i: (i, 0))
    args, specs = [h2], [row(d)]
    for mx in mixes:
        args.append(mx)
        specs.append(row(mx.shape[1]))
    for w in wos:
        args.append(w.astype(BF16))
        specs.append(_const_spec(w.shape))
    if bo is not None:
        args.append(bo.reshape(1, d))
        specs.append(_const_spec((1, d)))
    for a in (g2.reshape(1, d), wg.astype(BF16), wu.astype(BF16), wd.astype(BF16)):
        args.append(a)
        specs.append(_const_spec(a.shape))
    if final_g is not None:
        args.append(final_g.reshape(1, d))
        specs.append(_const_spec((1, d)))
    return pl.pallas_call(
        functools.partial(_post_kernel, n_mix=len(mixes), has_bias=bo is not None,
                          has_final=final_g is not None),
        grid=(m // tm,),
        in_specs=specs,
        out_specs=row(d),
        out_shape=jax.ShapeDtypeStruct((m, d), F32),
        compiler_params=pltpu.CompilerParams(dimension_semantics=("parallel",),
                                             vmem_limit_bytes=VMEM_LIMIT),
        name="post_ffn",
    )(*args)


def _rotate_half_columns(w):
    d, n = w.shape
    w4 = w.reshape(d, n // HEAD_DIM, 2, HEAD_DIM // 2)
    return jnp.concatenate([-w4[:, :, 1:2], w4[:, :, 0:1]], axis=2).reshape(d, n)


def _even_layout(w_in, mu, w_up, a_up):
    rw = 512
    q, k, v, g = (w_in[:, i * rw:(i + 1) * rw] for i in range(4))
    rest = w_in[:, 4 * rw:]
    w = jnp.concatenate([q, _rotate_half_columns(q), k, _rotate_half_columns(k), v, g, rest], axis=1)
    names = ["q", "q_rot", "k", "k_rot", "v", "g", "rr", "kr", "vr", "lora"]
    col = {nm: i * (rw // LANES) for i, nm in enumerate(names)}
    lora = 64
    zeros = jnp.zeros((lora, w_up.shape[1]), w_up.dtype)
    params = {
        "mu_r": mu[:rw], "mu_k": mu[rw:2 * rw], "mu_v": mu[2 * rw:3 * rw], "mu_l": mu[3 * rw:],
        "w_up": jnp.concatenate([w_up, zeros], axis=0),
        "a_up": jnp.concatenate([zeros, a_up], axis=0),
    }
    return w, col, params


def _swa_layout(w_qkv, b_qkv, n_heads=16, n_kv=4):
    qw = n_heads * HEAD_DIM
    kw = n_kv * HEAD_DIM

    def dup(x):
        lead = x.shape[:-1]
        x4 = x.reshape(lead + (n_kv, 1, HEAD_DIM))
        return jnp.broadcast_to(x4, lead + (n_kv, PAIR, HEAD_DIM)).reshape(lead + (n_kv * LANES,))

    w = jnp.concatenate([w_qkv[:, :qw], dup(w_qkv[:, qw:qw + kw]), dup(w_qkv[:, qw + kw:])], axis=1)
    b = jnp.concatenate([b_qkv[:qw], dup(b_qkv[qw:qw + kw]), dup(b_qkv[qw + kw:])], axis=0)
    return w, b


def _rope_tables(seq, dtype):
    half = HEAD_DIM // 2
    inv_freq = ROPE_BASE ** (-jnp.linspace(0.0, 1.0, half, dtype=dtype))
    ang = jnp.arange(seq, dtype=dtype)[:, None] * inv_freq[None, :]
    reps = LANES // half
    return jnp.tile(jnp.cos(ang), (1, reps)), jnp.tile(jnp.sin(ang), (1, reps))


def kernel(x, norm1_g, norm2_g, final_g, even_w_in, even_w_out, rwkv_mu, rwkv_w0, rwkv_w_up,
           rwkv_a0, rwkv_a_up, rwkv_g_up, rwkv_k_k, rwkv_k_a, rwkv_r_k, rwkv_ln_g, rwkv_ln_b,
           swa_w_qkv, swa_b_qkv, swa_sinks, swa_w_o, swa_b_o, ffn_w_gate, ffn_w_up, ffn_w_down):
    batch, seq, d = x.shape
    depth = norm1_g.shape[0]
    h = x.reshape(batch * seq, d)
    cos, sin = _rope_tables(seq, x.dtype)
    sinks_flat = swa_sinks.reshape(-1)
    ret_w = 512
    for layer in range(depth):
        i = layer // 2
        last = layer == depth - 1
        if layer % 2 == 0:
            w, col, params = _even_layout(even_w_in[i], rwkv_mu[i], rwkv_w_up[i], rwkv_a_up[i])
            params.update(w0=rwkv_w0[i], a0=rwkv_a0[i], g_up=rwkv_g_up[i], k_k=rwkv_k_k[i],
                          k_a=rwkv_k_a[i], r_k=rwkv_r_k[i].reshape(-1),
                          ln_g=rwkv_ln_g[i].reshape(-1), ln_b=rwkv_ln_b[i].reshape(-1))
            z = _norm_proj(h, norm1_g[layer], w, None, F32)
            ret_out = _retention(z, cos, sin, batch, seq, col)
            rwkv_out = _rwkv(z, params, batch, seq, col)
            mixes = [ret_out, rwkv_out]
            wos = [even_w_out[i][:ret_w], even_w_out[i][ret_w:]]
            bo = None
        else:
            w, b = _swa_layout(swa_w_qkv[i], swa_b_qkv[i])
            z = _norm_proj(h, norm1_g[layer], w, b, BF16)
            mixes = [_swa(z, sinks_flat, i, batch, seq)]
            wos = [swa_w_o[i]]
            bo = swa_b_o[i]
        h = _post(h, mixes, wos, bo, norm2_g[layer], ffn_w_gate[layer], ffn_w_up[layer],
                  ffn_w_down[layer], final_g if last else None)
    return h.reshape(batch, seq, d)
```

```python
import functools

import numpy as np
import jax
import jax.numpy as jnp
from jax import lax
from jax.experimental import pallas as pl
from jax.experimental.pallas import tpu as pltpu

F32 = jnp.float32
BF16 = jnp.bfloat16

LANES = 128
HEAD_DIM = 64
PAIR = LANES // HEAD_DIM
RMS_EPS = 1e-6
RET_GN_EPS = 1e-6
RWKV_GN_EPS = 64e-5
ROPE_BASE = 10000.0
RET_CHUNK = 128
RWKV_CHUNK = 64
SWA_WINDOW = 128
VMEM_LIMIT = 56 * 1024 * 1024


def _dot(a, b):
    return jnp.dot(a.astype(BF16), b.astype(BF16), preferred_element_type=F32)


def _dot_nt(a, b):
    return lax.dot_general(a.astype(BF16), b.astype(BF16), (((1,), (1,)), ((), ())),
                           preferred_element_type=F32)


def _split2(x):
    hi = x.astype(BF16)
    lo = (x - hi.astype(F32)).astype(BF16)
    return hi, lo


def _dot_hi2(x, m):
    hi, lo = _split2(x)
    return (jnp.dot(hi, m, preferred_element_type=F32)
            + jnp.dot(lo, m, preferred_element_type=F32))


def _dot_hi3_left(m, x):
    hi = x.astype(BF16)
    r1 = x - hi.astype(F32)
    mid = r1.astype(BF16)
    lo = (r1 - mid.astype(F32)).astype(BF16)
    return (jnp.dot(m, hi, preferred_element_type=F32)
            + jnp.dot(m, mid, preferred_element_type=F32)
            + jnp.dot(m, lo, preferred_element_type=F32))


def _iota2(shape, dim):
    return lax.broadcasted_iota(jnp.int32, shape, dim)


def _head_mask(rows):
    return _iota2((rows, LANES), 1) < HEAD_DIM


def _stack_heads(x, m0):
    zero = jnp.zeros_like(x)
    return jnp.concatenate([jnp.where(m0, x, zero), jnp.where(m0, zero, x)], axis=0)


def _unstack_heads(xs, m0):
    r = xs.shape[0] // 2
    return jnp.where(m0, xs[:r], xs[r:])


def _group_matrix(scale):
    r = _iota2((LANES, LANES), 0) // HEAD_DIM
    c = _iota2((LANES, LANES), 1) // HEAD_DIM
    return jnp.where(r == c, scale, 0.0).astype(BF16)


def _rms_norm(x, g):
    ms = jnp.mean(x * x, axis=-1, keepdims=True)
    return x * lax.rsqrt(ms + RMS_EPS) * g


def _norm_proj_kernel(*refs, has_bias):
    if has_bias:
        h_ref, g_ref, w_ref, b_ref, o_ref = refs
    else:
        h_ref, g_ref, w_ref, o_ref = refs
    n = _rms_norm(h_ref[...], g_ref[...])
    z = jnp.dot(n.astype(BF16), w_ref[...], preferred_element_type=F32)
    if has_bias:
        z = z + b_ref[...]
    o_ref[...] = z.astype(o_ref.dtype)


def _const_spec(shape):
    nd = len(shape)
    return pl.BlockSpec(shape, lambda *_: (0,) * nd, pipeline_mode=pl.Buffered(1))


def _norm_proj(h2, g, w, bias, out_dtype, tm=512):
    m, d = h2.shape
    n = w.shape[1]
    in_specs = [pl.BlockSpec((tm, d), lambda i: (i, 0)), _const_spec((1, d)), _const_spec((d, n))]
    args = [h2, g.reshape(1, d), w.astype(BF16)]
    if bias is not None:
        in_specs.append(_const_spec((1, n)))
        args.append(bias.reshape(1, n))
    return pl.pallas_call(
        functools.partial(_norm_proj_kernel, has_bias=bias is not None),
        grid=(m // tm,),
        in_specs=in_specs,
        out_specs=pl.BlockSpec((tm, n), lambda i: (i, 0)),
        out_shape=jax.ShapeDtypeStruct((m, n), out_dtype),
        compiler_params=pltpu.CompilerParams(dimension_semantics=("parallel",),
                                             vmem_limit_bytes=VMEM_LIMIT),
        name="norm_proj",
    )(*args)


def _retention_kernel(zq_ref, zk_ref, zv_ref, zg_ref, cos_ref, sin_ref,
                      dmask_ref, qdec_ref, kdec_ref, cdec_ref, o_ref, s_ref, *, n_chunks):
    @pl.when(pl.program_id(2) == 0)
    def _():
        s_ref[...] = jnp.zeros_like(s_ref)

    c = RET_CHUNK
    half = HEAD_DIM // 2
    m0 = _head_mask(c)
    m0_qk = (_iota2((c, LANES), 1) // half) % PAIR == 0
    gmean = _group_matrix(1.0 / HEAD_DIM)
    key_head = (_iota2((LANES, LANES), 0) // half) % PAIR
    value_head = _iota2((LANES, LANES), 1) // HEAD_DIM
    same_head = key_head == value_head
    dmask = dmask_ref[0]
    qdec = qdec_ref[0]
    kdec = kdec_ref[0]
    cdec = cdec_ref[0]

    def swap_halves(x):
        return pltpu.roll(x, HEAD_DIM, axis=1)

    def inner_terms(j):
        rows = slice(j * c, (j + 1) * c)
        cos = cos_ref[rows, :]
        sin = sin_ref[rows, :]
        zq = zq_ref[rows, :]
        zk = zk_ref[rows, :]
        q = zq * cos + swap_halves(zq) * sin
        k = (zk * cos + swap_halves(zk) * sin) * (HEAD_DIM ** -0.5)
        v = zv_ref[rows, :].astype(BF16)
        scores = _dot_nt(_stack_heads(q, m0_qk), k) * dmask
        kv = jnp.where(same_head, _dot((k * kdec).T, v), 0.0)
        yield
        o_inner = _unstack_heads(_dot(scores, v), m0)
        return (q * qdec).astype(BF16), o_inner, kv

    inner = _lockstep(inner_terms(j) for j in range(n_chunks))

    state = s_ref[...]
    states = []
    for _, _, kv in inner:
        states.append(state)
        state = cdec * state + kv
    s_ref[...] = state

    def finish(j):
        rows = slice(j * c, (j + 1) * c)
        q_dec, o_inner, _ = inner[j]
        o = o_inner + _dot(q_dec, states[j])
        yield
        mu = _dot_hi2(o, gmean)
        yield
        d = o - mu
        var = _dot_hi2(d * d, gmean)
        yield
        g = zg_ref[rows, :]
        out = d * lax.rsqrt(var + RET_GN_EPS) * (g * jax.nn.sigmoid(g))
        o_ref[rows, :] = out.astype(o_ref.dtype)

    _lockstep(finish(j) for j in range(n_chunks))


def _retention_tables(dtype):
    c = RET_CHUNK
    n_heads = 8
    h = jnp.arange(n_heads, dtype=dtype)
    log_gamma = jnp.log1p(-(2.0 ** (-5.0 - h)))
    idx = jnp.arange(c, dtype=dtype)
    rel = idx[:, None] - idx[None, :]
    inner = jnp.where(rel >= 0, jnp.exp(jnp.maximum(rel, 0.0)[None] * log_gamma[:, None, None]), 0.0)
    dmask = inner.reshape(n_heads // PAIR, PAIR * c, c)
    pair_lg = log_gamma.reshape(n_heads // PAIR, PAIR)
    lanes_lg = jnp.repeat(pair_lg, HEAD_DIM, axis=1)
    qk_lg = jnp.tile(jnp.repeat(pair_lg, HEAD_DIM // 2, axis=1), (1, 2))
    qdec = jnp.exp((idx + 1.0)[None, :, None] * qk_lg[:, None, :])
    kdec = jnp.exp((c - 1 - idx)[None, :, None] * qk_lg[:, None, :])
    cdec = jnp.broadcast_to(jnp.exp(c * lanes_lg)[:, None, :], (n_heads // PAIR, LANES, LANES))
    return dmask, qdec, kdec, cdec


def _retention(z, cos, sin, batch, seq, col, tb=1024):
    nt = seq // tb
    n_pairs = 4
    dmask, qdec, kdec, cdec = _retention_tables(z.dtype)

    def zspec(name):
        off = col[name]
        return pl.BlockSpec((tb, LANES), lambda b, p, t: (b * nt + t, off + p))

    tab = lambda shape: pl.BlockSpec((1,) + shape, lambda b, p, t: (p, 0, 0))
    rope = pl.BlockSpec((tb, LANES), lambda b, p, t: (t, 0))
    return pl.pallas_call(
        functools.partial(_retention_kernel, n_chunks=tb // RET_CHUNK),
        grid=(batch, n_pairs, nt),
        in_specs=[zspec("q"), zspec("k"), zspec("v"), zspec("g"),
                  rope, rope,
                  tab((PAIR * RET_CHUNK, RET_CHUNK)), tab((RET_CHUNK, LANES)),
                  tab((RET_CHUNK, LANES)), tab((LANES, LANES))],
        out_specs=pl.BlockSpec((tb, LANES), lambda b, p, t: (b * nt + t, p)),
        out_shape=jax.ShapeDtypeStruct((batch * seq, n_pairs * LANES), BF16),
        scratch_shapes=[pltpu.VMEM((LANES, LANES), F32)],
        compiler_params=pltpu.CompilerParams(
            dimension_semantics=("parallel", "parallel", "arbitrary"),
            vmem_limit_bytes=VMEM_LIMIT),
        name="retention",
    )(z, z, z, z, cos, sin, dmask, qdec, kdec, cdec)


def _lockstep(gens):
    gens = list(gens)
    results = [None] * len(gens)
    live = list(range(len(gens)))
    while live:
        still = []
        for i in live:
            try:
                next(gens[i])
                still.append(i)
            except StopIteration as stop:
                results[i] = stop.value
        live = still
    return results


def _inverse_masks():
    r = _iota2((LANES, LANES), 0)
    c = _iota2((LANES, LANES), 1)
    eye = jnp.where(r == c, 1.0, 0.0).astype(F32)
    diag8 = r // 8 == c // 8
    lower_left = [(r // (2 * k) == c // (2 * k)) & (r // k > c // k) for k in (8, 16, 32)]
    return eye, diag8, lower_left


def _unit_lower_inverse(n_mat, masks):
    eye, diag8, lower_left = masks
    p = jnp.where(diag8, n_mat, 0.0)
    t = eye + p
    p = _dot(p, p)
    yield
    t = t + _dot(t, p)
    p = _dot(p, p)
    yield
    t = t + _dot(t, p)
    yield
    for mask in lower_left:
        lt = _dot(jnp.where(mask, n_mat, 0.0), t)
        yield
        t = t + _dot(t, lt)
        yield
    return t


def _delayed(gen, rounds):
    for _ in range(rounds):
        yield
    yield from gen


def _rwkv_kernel(fr_ref, fk_ref, fv_ref, fl_ref, nr_ref, nk_ref, nv_ref, nl_ref,
                 mur_ref, muk_ref, muv_ref, mul_ref,
                 w0_ref, wup_ref, a0_ref, aup_ref, gup_ref, kk_ref, ka_ref, rk_ref, lng_ref, lnb_ref,
                 o_ref,
                 ht_ref, prev_ref, prevl_ref, r_s, lw_s, k_s, v_s, a_s, b_s, bonus_s, gate_s,
                 rhat_s, yhat_s, m_s, g_s, gtot_s, y_s, *, tb):
    t_idx = pl.program_id(2)
    c = RWKV_CHUNK
    n_chunks = tb // c
    gsum = _group_matrix(1.0)
    gmean = _group_matrix(1.0 / HEAD_DIM)
    first_row = _iota2((c, 1), 0) == 0

    def prepare(zr_ref, zk_ref, zv_ref, zl_ref, slot):
        for j in range(n_chunks):
            rows = slice(j * c, (j + 1) * c)

            def mixed(z_ref, mu_ref, carry_ref, carry_row):
                z = z_ref[rows, :]
                before = carry_ref[carry_row:carry_row + 1, :] if j == 0 else z_ref[j * c - 1:j * c, :]
                z_prev = jnp.where(first_row, before, pltpu.roll(z, 1, axis=0))
                return z + mu_ref[...] * (z_prev - z)

            xr = mixed(zr_ref, mur_ref, prev_ref, 0)
            xk = mixed(zk_ref, muk_ref, prev_ref, 1)
            xv = mixed(zv_ref, muv_ref, prev_ref, 2)
            xl = mixed(zl_ref, mul_ref, prevl_ref, 0)
            x_wa = xl[:, :LANES]
            w_pre = _dot(jnp.tanh(x_wa), wup_ref[...])
            a_pre = _dot(x_wa, aup_ref[...])
            gate = _dot(jax.nn.sigmoid(xl[:, LANES:]), gup_ref[...])
            kkf = xk * kk_ref[...]
            norm2 = _dot_hi2(kkf * kkf, gsum)
            yield
            w_log = -jax.nn.softplus(-(w0_ref[...] + w_pre)) - 0.5
            a = jax.nn.sigmoid(a0_ref[...] + a_pre)
            kk = kkf / jnp.maximum(jnp.sqrt(norm2), 1e-12)
            k2 = xk * (1.0 + (a - 1.0) * ka_ref[...])
            bonus_s[slot, rows, :] = _dot_hi2(xr * k2 * rk_ref[...], gsum) * xv
            gate_s[slot, rows, :] = gate
            r_s[rows, :] = xr
            lw_s[rows, :] = -jnp.exp(w_log)
            k_s[rows, :] = k2
            v_s[rows, :] = xv
            a_s[rows, :] = -kk
            b_s[rows, :] = kk * a
            yield
        prev_ref[0:1, :] = zr_ref[tb - 1:tb, :]
        prev_ref[1:2, :] = zk_ref[tb - 1:tb, :]
        prev_ref[2:3, :] = zv_ref[tb - 1:tb, :]
        prevl_ref[0:1, :] = zl_ref[tb - 1:tb, :]

    @pl.when(t_idx == 0)
    def _():
        for ref in (ht_ref, prev_ref, prevl_ref, bonus_s, gate_s, rhat_s, yhat_s, m_s, g_s, gtot_s):
            ref[...] = jnp.zeros_like(ref)
        for _ in prepare(fr_ref, fk_ref, fv_ref, fl_ref, 0):
            pass

    slot_next = lax.rem(t_idx + 1, 3)
    slot_done = lax.rem(t_idx + 2, 3)

    m0 = _head_mask(c)
    ri = _iota2((LANES, LANES), 0)
    ci = _iota2((LANES, LANES), 1)
    same_head = ri // c == ci // c
    strict = same_head & (ri > ci)
    incl = same_head & (ri >= ci)
    tri = jnp.where(_iota2((c, c), 0) >= _iota2((c, c), 1), 1.0, 0.0).astype(BF16)

    inv_masks = _inverse_masks()

    def chunk_terms(j):
        rows = slice(j * c, (j + 1) * c)
        lw = lw_s[rows, :]
        cum = _dot_hi3_left(tri, lw)
        yield
        cum_last = cum[c - 1:c, :]
        g_inv = jnp.exp(-cum)
        g_rem = jnp.exp(cum_last - cum)
        a_st = _stack_heads(a_s[rows, :] * jnp.exp(cum - lw), m0)
        r_st = _stack_heads(r_s[rows, :] * jnp.exp(cum), m0)
        v_st = _stack_heads(v_s[rows, :], m0)
        bg_st = _stack_heads(b_s[rows, :] * g_rem, m0)
        kg_st = _stack_heads(k_s[rows, :] * g_rem, m0)
        b_t = (b_s[rows, :] * g_inv).astype(BF16)
        k_t = (k_s[rows, :] * g_inv).astype(BF16)
        b2 = jnp.concatenate([b_t, b_t], axis=0)
        k2_ = jnp.concatenate([k_t, k_t], axis=0)
        a_bf = a_st.astype(BF16)
        r_bf = r_st.astype(BF16)
        v_bf = v_st.astype(BF16)
        n_ab = jnp.where(strict, _dot_nt(a_bf, b2), 0.0)
        n_ak = jnp.where(strict, _dot_nt(a_bf, k2_), 0.0)
        n_rb = jnp.where(incl, _dot_nt(r_bf, b2), 0.0)
        n_rk = jnp.where(incl, _dot_nt(r_bf, k2_), 0.0)
        yield
        akv = _dot(n_ak, v_bf)
        rkv = _dot(n_rk, v_bf)
        vk = _dot(v_st.T, kg_st)
        t_inv = yield from _unit_lower_inverse(n_ab, inv_masks)
        wu = _dot(t_inv, jnp.concatenate([a_st, akv], axis=1))
        yield
        rw = _dot(n_rb, wu)
        mg = _dot(wu.T, bg_st)
        yield
        rhat_s[j] = (r_st + rw[:, :LANES]).astype(BF16)
        yhat_s[j] = rw[:, LANES:] + rkv
        m_s[j] = mg[:LANES].astype(BF16)
        g_s[j] = mg[LANES:] + vk
        gtot_s[8 * j:8 * j + 1, :] = jnp.exp(cum_last)

    def chain_and_output():
        ht = ht_ref[...]
        for j in range(n_chunks):
            y_st = _dot_nt(rhat_s[j], ht) + yhat_s[j]
            y_s[j * c:(j + 1) * c, :] = y_st[:c] + y_st[c:]
            ht = ht * gtot_s[8 * j:8 * j + 1, :] + _dot(ht, m_s[j]) + g_s[j]
            yield
        ht_ref[...] = ht
        y = y_s[...]
        mu = _dot_hi2(y, gmean)
        yield
        d = y - mu
        var = _dot_hi2(d * d, gmean)
        yield
        yn = d * lax.rsqrt(var + RWKV_GN_EPS) * lng_ref[...] + lnb_ref[...]
        o_ref[...] = ((yn + bonus_s[slot_done]) * gate_s[slot_done]).astype(o_ref.dtype)

    _lockstep([chunk_terms(j) for j in range(n_chunks)]
              + [chain_and_output(), _delayed(prepare(nr_ref, nk_ref, nv_ref, nl_ref, slot_next), 2)])


def _rwkv(z, params, batch, seq, col, tb=512):
    nt = seq // tb
    n_pairs = 4

    def zspec(name, block_of_step, width=LANES):
        off = col[name]
        if width == LANES:
            return pl.BlockSpec((tb, LANES), lambda b, p, t: (b * nt + block_of_step(t), off + p))
        return pl.BlockSpec((tb, width), lambda b, p, t: (b * nt + block_of_step(t), off * LANES // width))

    first_block = lambda t: 0
    next_block = lambda t: jnp.minimum(t + 1, nt - 1)

    def pair_row(x):
        return x.reshape(1, -1), pl.BlockSpec((1, LANES), lambda b, p, t: (0, p))

    def pair_cols(x):
        return x.astype(BF16), pl.BlockSpec((x.shape[0], LANES), lambda b, p, t: (0, p))

    def whole_row(x):
        return x.reshape(1, -1), pl.BlockSpec((1, x.size), lambda b, p, t: (0, 0))

    args, specs = [z] * 8, []
    for block_of_step in (first_block, next_block):
        specs += [zspec("rr", block_of_step), zspec("kr", block_of_step), zspec("vr", block_of_step),
                  zspec("lora", block_of_step, 2 * LANES)]
    for a, s in (pair_row(params["mu_r"]), pair_row(params["mu_k"]), pair_row(params["mu_v"]),
                 whole_row(params["mu_l"]),
                 pair_row(params["w0"]), pair_cols(params["w_up"]),
                 pair_row(params["a0"]), pair_cols(params["a_up"]), pair_cols(params["g_up"]),
                 pair_row(params["k_k"]), pair_row(params["k_a"]), pair_row(params["r_k"]),
                 pair_row(params["ln_g"]), pair_row(params["ln_b"])):
        args.append(a)
        specs.append(s)

    n_chunks = tb // RWKV_CHUNK
    blk = pltpu.VMEM((tb, LANES), F32)
    ring = pltpu.VMEM((3, tb, LANES), F32)
    mat = lambda dtype: pltpu.VMEM((n_chunks, LANES, LANES), dtype)
    return pl.pallas_call(
        functools.partial(_rwkv_kernel, tb=tb),
        grid=(batch, n_pairs, nt + 1),
        in_specs=specs,
        out_specs=pl.BlockSpec((tb, LANES), lambda b, p, t: (b * nt + jnp.maximum(t - 1, 0), p)),
        out_shape=jax.ShapeDtypeStruct((batch * seq, n_pairs * LANES), BF16),
        scratch_shapes=[pltpu.VMEM((LANES, LANES), F32),
                        pltpu.VMEM((8, LANES), F32),
                        pltpu.VMEM((8, 2 * LANES), F32),
                        blk, blk, blk, blk, blk, blk,
                        ring, ring,
                        mat(BF16), mat(F32), mat(BF16), mat(F32),
                        pltpu.VMEM((8 * n_chunks, LANES), F32),
                        blk],
        compiler_params=pltpu.CompilerParams(
            dimension_semantics=("parallel", "parallel", "arbitrary"),
            vmem_limit_bytes=VMEM_LIMIT),
        name="rwkv7",
    )(*args)


def _swa_kernel(sink_ref, q_ref, kv_ref, kvp_ref, o_ref, *, n_qblocks, layer_sink_base):
    n = pl.program_id(1)
    w = SWA_WINDOW
    m0 = _head_mask(w)
    qi = _iota2((PAIR * w, 2 * w), 0) % w
    kj = _iota2((PAIR * w, 2 * w), 1)
    rel = qi + w - kj
    first_key = jnp.where(n > 0, 0, w)
    valid = (rel >= 0) & (rel < w) & (kj >= first_key)
    second_head = _iota2((PAIR * w, 1), 0) >= w
    kv = kv_ref[...]
    kvp = kvp_ref[...]
    n_kv_cols = kv.shape[1] // 2

    def qblock(qb):
        h = qb // 2
        cols = slice(qb * LANES, (qb + 1) * LANES)
        kcols = slice(h * LANES, (h + 1) * LANES)
        vcols = slice(n_kv_cols + h * LANES, n_kv_cols + (h + 1) * LANES)
        q = q_ref[:, cols] * (HEAD_DIM ** -0.5)
        kd = jnp.concatenate([kvp[:, kcols], kv[:, kcols]], axis=0)
        vd = jnp.concatenate([kvp[:, vcols], kv[:, vcols]], axis=0)
        s = _dot_nt(_stack_heads(q, m0), kd)
        yield
        s = jnp.where(valid, s, -jnp.inf)
        sink = jnp.where(second_head, sink_ref[layer_sink_base + 2 * qb + 1],
                         sink_ref[layer_sink_base + 2 * qb])
        m = jnp.maximum(jnp.max(s, axis=-1, keepdims=True), sink)
        e = jnp.exp(s - m)
        denom = jnp.sum(e, axis=-1, keepdims=True) + jnp.exp(sink - m)
        pv = _dot(e, vd)
        yield
        o_ref[:, cols] = _unstack_heads(pv / denom, m0).astype(o_ref.dtype)

    _lockstep(qblock(qb) for qb in range(n_qblocks))


def _swa(z, sinks_flat, layer, batch, seq, n_heads=16):
    w = SWA_WINDOW
    nb = seq // w
    qw = n_heads * HEAD_DIM
    kvw = z.shape[1] - qw
    assert kvw == qw, "duplicated k|v block must be as wide as the q block"
    return pl.pallas_call(
        functools.partial(_swa_kernel, n_qblocks=qw // LANES, layer_sink_base=layer * n_heads),
        grid=(batch, nb),
        in_specs=[pl.BlockSpec(memory_space=pltpu.SMEM),
                  pl.BlockSpec((w, qw), lambda b, n: (b * nb + n, 0)),
                  pl.BlockSpec((w, kvw), lambda b, n: (b * nb + n, 1)),
                  pl.BlockSpec((w, kvw), lambda b, n: (b * nb + jnp.maximum(n - 1, 0), 1))],
        out_specs=pl.BlockSpec((w, qw), lambda b, n: (b * nb + n, 0)),
        out_shape=jax.ShapeDtypeStruct((batch * seq, qw), BF16),
        compiler_params=pltpu.CompilerParams(dimension_semantics=("parallel", "arbitrary"),
                                             vmem_limit_bytes=VMEM_LIMIT),
        name="swa",
    )(sinks_flat, z, z, z)


def _post_kernel(*refs, n_mix, has_bias, has_final):
    it = iter(refs)
    h_ref = next(it)
    mix_refs = [next(it) for _ in range(n_mix)]
    wo_refs = [next(it) for _ in range(n_mix)]
    bo_ref = next(it) if has_bias else None
    g2_ref, wg_ref, wu_ref, wd_ref = next(it), next(it), next(it), next(it)
    gf_ref = next(it) if has_final else None
    o_ref = next(it)

    h = h_ref[...]
    for m_ref, w_ref in zip(mix_refs, wo_refs):
        h = h + jnp.dot(m_ref[...].astype(BF16), w_ref[...], preferred_element_type=F32)
    if has_bias:
        h = h + bo_ref[...]
    n = _rms_norm(h, g2_ref[...]).astype(BF16)
    gate = jnp.dot(n, wg_ref[...], preferred_element_type=F32)
    up = jnp.dot(n, wu_ref[...], preferred_element_type=F32)
    act = (gate * jax.nn.sigmoid(gate) * up).astype(BF16)
    h = h + jnp.dot(act, wd_ref[...], preferred_element_type=F32)
    if has_final:
        h = _rms_norm(h, gf_ref[...])
    o_ref[...] = h


def _post(h2, mixes, wos, bo, g2, wg, wu, wd, final_g, tm=512):
    m, d = h2.shape
    row = lambda width: pl.BlockSpec((tm, width), lambda i: (i, 0))
    args, specs = [h2], [row(d)]
    for mx in mixes:
        args.append(mx)
        specs.append(row(mx.shape[1]))
    for w in wos:
        args.append(w.astype(BF16))
        specs.append(_const_spec(w.shape))
    if bo is not None:
        args.append(bo.reshape(1, d))
        specs.append(_const_spec((1, d)))
    for a in (g2.reshape(1, d), wg.astype(BF16), wu.astype(BF16), wd.astype(BF16)):
        args.append(a)
        specs.append(_const_spec(a.shape))
    if final_g is not None:
        args.append(final_g.reshape(1, d))
        specs.append(_const_spec((1, d)))
    return pl.pallas_call(
        functools.partial(_post_kernel, n_mix=len(mixes), has_bias=bo is not None,
                          has_final=final_g is not None),
        grid=(m // tm,),
        in_specs=specs,
        out_specs=row(d),
        out_shape=jax.ShapeDtypeStruct((m, d), F32),
        compiler_params=pltpu.CompilerParams(dimension_semantics=("parallel",),
                                             vmem_limit_bytes=VMEM_LIMIT),
        name="post_ffn",
    )(*args)


def _even_layout(w_in, mu, w_up, a_up):
    rw = 512
    d = w_in.shape[0]

    def interleave_halves(w):
        w5 = w.reshape(d, rw // LANES, PAIR, 2, HEAD_DIM // 2)
        return w5.transpose(0, 1, 3, 2, 4).reshape(d, rw)

    w = jnp.concatenate([interleave_halves(w_in[:, :rw]), interleave_halves(w_in[:, rw:2 * rw]),
                         w_in[:, 2 * rw:]], axis=1)
    names = ["q", "k", "v", "g", "rr", "kr", "vr", "lora"]
    col = {nm: i * (rw // LANES) for i, nm in enumerate(names)}
    lora = 64
    zeros = jnp.zeros((lora, w_up.shape[1]), w_up.dtype)
    params = {
        "mu_r": mu[:rw], "mu_k": mu[rw:2 * rw], "mu_v": mu[2 * rw:3 * rw], "mu_l": mu[3 * rw:],
        "w_up": jnp.concatenate([w_up, zeros], axis=0),
        "a_up": jnp.concatenate([zeros, a_up], axis=0),
    }
    return w, col, params


def _swa_layout(w_qkv, b_qkv, n_heads=16, n_kv=4):
    qw = n_heads * HEAD_DIM
    kw = n_kv * HEAD_DIM

    def dup(x):
        lead = x.shape[:-1]
        x4 = x.reshape(lead + (n_kv, 1, HEAD_DIM))
        return jnp.broadcast_to(x4, lead + (n_kv, PAIR, HEAD_DIM)).reshape(lead + (n_kv * LANES,))

    w = jnp.concatenate([w_qkv[:, :qw], dup(w_qkv[:, qw:qw + kw]), dup(w_qkv[:, qw + kw:])], axis=1)
    b = jnp.concatenate([b_qkv[:qw], dup(b_qkv[qw:qw + kw]), dup(b_qkv[qw + kw:])], axis=0)
    return w, b


def _rope_tables(seq, dtype):
    half = HEAD_DIM // 2
    inv_freq = ROPE_BASE ** (-jnp.linspace(0.0, 1.0, half, dtype=dtype))
    ang = jnp.arange(seq, dtype=dtype)[:, None] * inv_freq[None, :]
    sin = jnp.sin(ang)
    cos = jnp.cos(ang)
    return (jnp.tile(cos, (1, LANES // half)),
            jnp.concatenate([-sin] * PAIR + [sin] * PAIR, axis=1))


def kernel(x, norm1_g, norm2_g, final_g, even_w_in, even_w_out, rwkv_mu, rwkv_w0, rwkv_w_up,
           rwkv_a0, rwkv_a_up, rwkv_g_up, rwkv_k_k, rwkv_k_a, rwkv_r_k, rwkv_ln_g, rwkv_ln_b,
           swa_w_qkv, swa_b_qkv, swa_sinks, swa_w_o, swa_b_o, ffn_w_gate, ffn_w_up, ffn_w_down):
    batch, seq, d = x.shape
    depth = norm1_g.shape[0]
    h = x.reshape(batch * seq, d)
    cos, sin = _rope_tables(seq, x.dtype)
    sinks_flat = swa_sinks.reshape(-1)
    ret_w = 512
    for layer in range(depth):
        i = layer // 2
        last = layer == depth - 1
        if layer % 2 == 0:
            w, col, params = _even_layout(even_w_in[i], rwkv_mu[i], rwkv_w_up[i], rwkv_a_up[i])
            params.update(w0=rwkv_w0[i], a0=rwkv_a0[i], g_up=rwkv_g_up[i], k_k=rwkv_k_k[i],
                          k_a=rwkv_k_a[i], r_k=rwkv_r_k[i].reshape(-1),
                          ln_g=rwkv_ln_g[i].reshape(-1), ln_b=rwkv_ln_b[i].reshape(-1))
            z = _norm_proj(h, norm1_g[layer], w, None, F32)
            ret_out = _retention(z, cos, sin, batch, seq, col)
            rwkv_out = _rwkv(z, params, batch, seq, col)
            mixes = [ret_out, rwkv_out]
            wos = [even_w_out[i][:ret_w], even_w_out[i][ret_w:]]
            bo = None
        else:
            w, b = _swa_layout(swa_w_qkv[i], swa_b_qkv[i])
            z = _norm_proj(h, norm1_g[layer], w, b, BF16)
            mixes = [_swa(z, sinks_flat, i, batch, seq)]
            wos = [swa_w_o[i]]
            bo = swa_b_o[i]
        h = _post(h, mixes, wos, bo, norm2_g[layer], ffn_w_gate[layer], ffn_w_up[layer],
                  ffn_w_down[layer], final_g if last else None)
    return h.reshape(batch, seq, d)
```

```python
import functools

import numpy as np
import jax
import jax.numpy as jnp
from jax import lax
from jax.experimental import pallas as pl
from jax.experimental.pallas import tpu as pltpu

F32 = jnp.float32
BF16 = jnp.bfloat16

LANES = 128
HEAD_DIM = 64
PAIR = LANES // HEAD_DIM
RMS_EPS = 1e-6
RET_GN_EPS = 1e-6
RWKV_GN_EPS = 64e-5
ROPE_BASE = 10000.0
RET_CHUNK = 128
RWKV_CHUNK = 64
SWA_WINDOW = 128
VMEM_LIMIT = 56 * 1024 * 1024


def _dot(a, b):
    return jnp.dot(a.astype(BF16), b.astype(BF16), preferred_element_type=F32)


def _dot_nt(a, b):
    return lax.dot_general(a.astype(BF16), b.astype(BF16), (((1,), (1,)), ((), ())),
                           preferred_element_type=F32)


def _split2(x):
    hi = x.astype(BF16)
    lo = (x - hi.astype(F32)).astype(BF16)
    return hi, lo


def _dot_hi2(x, m):
    hi, lo = _split2(x)
    return (jnp.dot(hi, m, preferred_element_type=F32)
            + jnp.dot(lo, m, preferred_element_type=F32))


def _dot_hi3_left(m, x):
    hi = x.astype(BF16)
    r1 = x - hi.astype(F32)
    mid = r1.astype(BF16)
    lo = (r1 - mid.astype(F32)).astype(BF16)
    return (jnp.dot(m, hi, preferred_element_type=F32)
            + jnp.dot(m, mid, preferred_element_type=F32)
            + jnp.dot(m, lo, preferred_element_type=F32))


def _iota2(shape, dim):
    return lax.broadcasted_iota(jnp.int32, shape, dim)


def _head_mask(rows):
    return _iota2((rows, LANES), 1) < HEAD_DIM


def _stack_heads(x, m0):
    zero = jnp.zeros_like(x)
    return jnp.concatenate([jnp.where(m0, x, zero), jnp.where(m0, zero, x)], axis=0)


def _unstack_heads(xs, m0):
    r = xs.shape[0] // 2
    return jnp.where(m0, xs[:r], xs[r:])


def _group_matrix(scale):
    r = _iota2((LANES, LANES), 0) // HEAD_DIM
    c = _iota2((LANES, LANES), 1) // HEAD_DIM
    return jnp.where(r == c, scale, 0.0).astype(BF16)


def _rms_norm(x, g):
    ms = jnp.mean(x * x, axis=-1, keepdims=True)
    return x * lax.rsqrt(ms + RMS_EPS) * g


def _norm_proj_kernel(*refs, has_bias):
    if has_bias:
        h_ref, g_ref, w_ref, b_ref, o_ref = refs
    else:
        h_ref, g_ref, w_ref, o_ref = refs
    n = _rms_norm(h_ref[...], g_ref[...])
    z = jnp.dot(n.astype(BF16), w_ref[...], preferred_element_type=F32)
    if has_bias:
        z = z + b_ref[...]
    o_ref[...] = z.astype(o_ref.dtype)


def _const_spec(shape):
    nd = len(shape)
    return pl.BlockSpec(shape, lambda *_: (0,) * nd, pipeline_mode=pl.Buffered(1))


def _norm_proj(h2, g, w, bias, out_dtype, tm=512):
    m, d = h2.shape
    n = w.shape[1]
    in_specs = [pl.BlockSpec((tm, d), lambda i: (i, 0)), _const_spec((1, d)), _const_spec((d, n))]
    args = [h2, g.reshape(1, d), w.astype(BF16)]
    if bias is not None:
        in_specs.append(_const_spec((1, n)))
        args.append(bias.reshape(1, n))
    return pl.pallas_call(
        functools.partial(_norm_proj_kernel, has_bias=bias is not None),
        grid=(m // tm,),
        in_specs=in_specs,
        out_specs=pl.BlockSpec((tm, n), lambda i: (i, 0)),
        out_shape=jax.ShapeDtypeStruct((m, n), out_dtype),
        compiler_params=pltpu.CompilerParams(dimension_semantics=("parallel",),
                                             vmem_limit_bytes=VMEM_LIMIT),
        name="norm_proj",
    )(*args)


def _retention_kernel(zq_ref, zk_ref, zv_ref, zg_ref, cos_ref, sin_ref,
                      dmask_ref, qdec_ref, kdec_ref, cdec_ref, o_ref, s_ref, *, n_chunks):
    @pl.when(pl.program_id(2) == 0)
    def _():
        s_ref[...] = jnp.zeros_like(s_ref)

    c = RET_CHUNK
    half = HEAD_DIM // 2
    m0 = _head_mask(c)
    m0_qk = (_iota2((c, LANES), 1) // half) % PAIR == 0
    gmean = _group_matrix(1.0 / HEAD_DIM)
    key_head = (_iota2((LANES, LANES), 0) // half) % PAIR
    value_head = _iota2((LANES, LANES), 1) // HEAD_DIM
    same_head = key_head == value_head
    dmask = dmask_ref[0]
    qdec = qdec_ref[0]
    kdec = kdec_ref[0]
    cdec = cdec_ref[0]

    def swap_halves(x):
        return pltpu.roll(x, HEAD_DIM, axis=1)

    def inner_terms(j):
        rows = slice(j * c, (j + 1) * c)
        cos = cos_ref[rows, :]
        sin = sin_ref[rows, :]
        zq = zq_ref[rows, :]
        zk = zk_ref[rows, :]
        q = zq * cos + swap_halves(zq) * sin
        k = (zk * cos + swap_halves(zk) * sin) * (HEAD_DIM ** -0.5)
        v = zv_ref[rows, :].astype(BF16)
        scores = _dot_nt(_stack_heads(q, m0_qk), k) * dmask
        kv = jnp.where(same_head, _dot((k * kdec).T, v), 0.0)
        yield
        o_inner = _unstack_heads(_dot(scores, v), m0)
        return (q * qdec).astype(BF16), o_inner, kv

    inner = _lockstep(inner_terms(j) for j in range(n_chunks))

    state = s_ref[...]
    states = []
    for _, _, kv in inner:
        states.append(state)
        state = cdec * state + kv
    s_ref[...] = state

    def finish(j):
        rows = slice(j * c, (j + 1) * c)
        q_dec, o_inner, _ = inner[j]
        o = o_inner + _dot(q_dec, states[j])
        yield
        mu = _dot_hi2(o, gmean)
        yield
        d = o - mu
        var = _dot_hi2(d * d, gmean)
        yield
        g = zg_ref[rows, :]
        out = d * lax.rsqrt(var + RET_GN_EPS) * (g * jax.nn.sigmoid(g))
        o_ref[rows, :] = out.astype(o_ref.dtype)

    _lockstep(finish(j) for j in range(n_chunks))


def _retention_tables(dtype):
    c = RET_CHUNK
    n_heads = 8
    h = jnp.arange(n_heads, dtype=dtype)
    log_gamma = jnp.log1p(-(2.0 ** (-5.0 - h)))
    idx = jnp.arange(c, dtype=dtype)
    rel = idx[:, None] - idx[None, :]
    inner = jnp.where(rel >= 0, jnp.exp(jnp.maximum(rel, 0.0)[None] * log_gamma[:, None, None]), 0.0)
    dmask = inner.reshape(n_heads // PAIR, PAIR * c, c)
    pair_lg = log_gamma.reshape(n_heads // PAIR, PAIR)
    lanes_lg = jnp.repeat(pair_lg, HEAD_DIM, axis=1)
    qk_lg = jnp.tile(jnp.repeat(pair_lg, HEAD_DIM // 2, axis=1), (1, 2))
    qdec = jnp.exp((idx + 1.0)[None, :, None] * qk_lg[:, None, :])
    kdec = jnp.exp((c - 1 - idx)[None, :, None] * qk_lg[:, None, :])
    cdec = jnp.broadcast_to(jnp.exp(c * lanes_lg)[:, None, :], (n_heads // PAIR, LANES, LANES))
    return dmask, qdec, kdec, cdec


def _retention(z, cos, sin, batch, seq, col, tb=1024):
    nt = seq // tb
    n_pairs = 4
    dmask, qdec, kdec, cdec = _retention_tables(z.dtype)

    def zspec(name):
        off = col[name]
        return pl.BlockSpec((tb, LANES), lambda b, p, t: (b * nt + t, off + p))

    tab = lambda shape: pl.BlockSpec((1,) + shape, lambda b, p, t: (p, 0, 0))
    rope = pl.BlockSpec((tb, LANES), lambda b, p, t: (t, 0))
    return pl.pallas_call(
        functools.partial(_retention_kernel, n_chunks=tb // RET_CHUNK),
        grid=(batch, n_pairs, nt),
        in_specs=[zspec("q"), zspec("k"), zspec("v"), zspec("g"),
                  rope, rope,
                  tab((PAIR * RET_CHUNK, RET_CHUNK)), tab((RET_CHUNK, LANES)),
                  tab((RET_CHUNK, LANES)), tab((LANES, LANES))],
        out_specs=pl.BlockSpec((tb, LANES), lambda b, p, t: (b * nt + t, p)),
        out_shape=jax.ShapeDtypeStruct((batch * seq, n_pairs * LANES), BF16),
        scratch_shapes=[pltpu.VMEM((LANES, LANES), F32)],
        compiler_params=pltpu.CompilerParams(
            dimension_semantics=("parallel", "parallel", "arbitrary"),
            vmem_limit_bytes=VMEM_LIMIT),
        name="retention",
    )(z, z, z, z, cos, sin, dmask, qdec, kdec, cdec)


def _lockstep(gens):
    gens = list(gens)
    results = [None] * len(gens)
    live = list(range(len(gens)))
    while live:
        still = []
        for i in live:
            try:
                next(gens[i])
                still.append(i)
            except StopIteration as stop:
                results[i] = stop.value
        live = still
    return results


def _inverse_masks():
    r = _iota2((LANES, LANES), 0)
    c = _iota2((LANES, LANES), 1)
    eye = jnp.where(r == c, 1.0, 0.0).astype(F32)
    diag8 = r // 8 == c // 8
    lower_left = [(r // (2 * k) == c // (2 * k)) & (r // k > c // k) for k in (8, 16, 32)]
    return eye, diag8, lower_left


def _unit_lower_inverse(n_mat, masks):
    eye, diag8, lower_left = masks
    p = jnp.where(diag8, n_mat, 0.0)
    t = eye + p
    p = _dot(p, p)
    yield
    tp = _dot(jnp.concatenate([t, p], axis=0), p)
    t = t + tp[:LANES]
    p = tp[LANES:]
    yield
    t = t + _dot(t, p)
    yield
    for mask in lower_left:
        lt = _dot(jnp.where(mask, n_mat, 0.0), t)
        yield
        t = t + _dot(t, lt)
        yield
    return t


def _delayed(gen, rounds):
    for _ in range(rounds):
        yield
    yield from gen


def _rwkv_kernel(fr_ref, fk_ref, fv_ref, fl_ref, nr_ref, nk_ref, nv_ref, nl_ref,
                 mur_ref, muk_ref, muv_ref, mul_ref,
                 w0_ref, wup_ref, a0_ref, aup_ref, gup_ref, kk_ref, ka_ref, rk_ref, lng_ref, lnb_ref,
                 o_ref,
                 ht_ref, prev_ref, prevl_ref, r_s, lw_s, k_s, v_s, a_s, b_s, bonus_s, gate_s,
                 rhat_s, yhat_s, m_s, g_s, gtot_s, y_s, *, tb):
    t_idx = pl.program_id(2)
    c = RWKV_CHUNK
    n_chunks = tb // c
    gsum = _group_matrix(1.0)
    gmean = _group_matrix(1.0 / HEAD_DIM)
    pr = 2 * c
    first_row = _iota2((pr, 1), 0) == 0

    def prepare(zr_ref, zk_ref, zv_ref, zl_ref, slot):
        def prepare_rows(j):
            rows = slice(j * pr, (j + 1) * pr)

            def mixed(z_ref, mu_ref, carry_ref, carry_row):
                z = z_ref[rows, :]
                before = carry_ref[carry_row:carry_row + 1, :] if j == 0 else z_ref[j * pr - 1:j * pr, :]
                z_prev = jnp.where(first_row, before, pltpu.roll(z, 1, axis=0))
                return z + mu_ref[...] * (z_prev - z)

            xr = mixed(zr_ref, mur_ref, prev_ref, 0)
            xk = mixed(zk_ref, muk_ref, prev_ref, 1)
            xv = mixed(zv_ref, muv_ref, prev_ref, 2)
            xl = mixed(zl_ref, mul_ref, prevl_ref, 0)
            x_wa = xl[:, :LANES]
            w_pre = _dot(jnp.tanh(x_wa), wup_ref[...])
            a_pre = _dot(x_wa, aup_ref[...])
            gate = _dot(jax.nn.sigmoid(xl[:, LANES:]), gup_ref[...])
            kkf = xk * kk_ref[...]
            norm2 = _dot_hi2(kkf * kkf, gsum)
            yield
            w_log = -jax.nn.softplus(-(w0_ref[...] + w_pre)) - 0.5
            a = jax.nn.sigmoid(a0_ref[...] + a_pre)
            kk = kkf / jnp.maximum(jnp.sqrt(norm2), 1e-12)
            k2 = xk * (1.0 + (a - 1.0) * ka_ref[...])
            rk_sum = _dot_hi2(xr * k2 * rk_ref[...], gsum)
            gate_s[slot, rows, :] = gate
            r_s[rows, :] = xr
            lw_s[rows, :] = -jnp.exp(w_log)
            k_s[rows, :] = k2
            v_s[rows, :] = xv
            a_s[rows, :] = -kk
            b_s[rows, :] = kk * a
            yield
            bonus_s[slot, rows, :] = rk_sum * xv

        waiting = [prepare_rows(j) for j in range(tb // pr)]
        running = []
        while waiting or running:
            if waiting:
                running.append(waiting.pop(0))
            running = [g for g in running if next(g, "done") != "done"]
            yield
        prev_ref[0:1, :] = zr_ref[tb - 1:tb, :]
        prev_ref[1:2, :] = zk_ref[tb - 1:tb, :]
        prev_ref[2:3, :] = zv_ref[tb - 1:tb, :]
        prevl_ref[0:1, :] = zl_ref[tb - 1:tb, :]

    @pl.when(t_idx == 0)
    def _():
        for ref in (ht_ref, prev_ref, prevl_ref, bonus_s, gate_s, rhat_s, yhat_s, m_s, g_s, gtot_s):
            ref[...] = jnp.zeros_like(ref)
        for _ in prepare(fr_ref, fk_ref, fv_ref, fl_ref, 0):
            pass

    slot_next = lax.rem(t_idx + 1, 3)
    slot_done = lax.rem(t_idx + 2, 3)

    m0 = _head_mask(c)
    ri = _iota2((LANES, LANES), 0)
    ci = _iota2((LANES, LANES), 1)
    same_head = ri // c == ci // c
    strict = same_head & (ri > ci)
    incl = same_head & (ri >= ci)
    tri = jnp.where(_iota2((c, c), 0) >= _iota2((c, c), 1), 1.0, 0.0).astype(BF16)

    inv_masks = _inverse_masks()

    def chunk_terms(j):
        rows = slice(j * c, (j + 1) * c)
        lw = lw_s[rows, :]
        cum = _dot_hi3_left(tri, lw)
        yield
        cum_last = cum[c - 1:c, :]
        g_inv = jnp.exp(-cum)
        g_rem = jnp.exp(cum_last - cum)
        a_st = _stack_heads(a_s[rows, :] * jnp.exp(cum - lw), m0)
        r_st = _stack_heads(r_s[rows, :] * jnp.exp(cum), m0)
        v_st = _stack_heads(v_s[rows, :], m0)
        bg_st = _stack_heads(b_s[rows, :] * g_rem, m0)
        kg_st = _stack_heads(k_s[rows, :] * g_rem, m0)
        b_t = (b_s[rows, :] * g_inv).astype(BF16)
        k_t = (k_s[rows, :] * g_inv).astype(BF16)
        b2 = jnp.concatenate([b_t, b_t], axis=0)
        k2_ = jnp.concatenate([k_t, k_t], axis=0)
        ar_bf = jnp.concatenate([a_st, r_st], axis=0).astype(BF16)
        v_bf = v_st.astype(BF16)
        ar_b = _dot_nt(ar_bf, b2)
        ar_k = _dot_nt(ar_bf, k2_)
        n_ab = jnp.where(strict, ar_b[:LANES], 0.0)
        n_ak = jnp.where(strict, ar_k[:LANES], 0.0)
        n_rb = jnp.where(incl, ar_b[LANES:], 0.0)
        n_rk = jnp.where(incl, ar_k[LANES:], 0.0)
        yield
        akv = _dot(n_ak, v_bf)
        bk_t = jnp.concatenate([bg_st, kg_st], axis=0).T.astype(BF16)
        t_inv = yield from _unit_lower_inverse(n_ab, inv_masks)
        wu = _dot(t_inv, jnp.concatenate([a_st, akv], axis=1))
        yield
        wu_v = jnp.concatenate(
            [wu.astype(BF16), jnp.concatenate([jnp.zeros_like(v_bf), v_bf], axis=1)], axis=0)
        lhs = jnp.concatenate([jnp.concatenate([n_rb, n_rk], axis=1).astype(BF16), bk_t], axis=0)
        both = jnp.dot(lhs, wu_v, preferred_element_type=F32)
        rw = both[:LANES]
        mg_t = both[LANES:]
        yield
        rhat_s[j] = (r_st + rw[:, :LANES]).astype(BF16)
        yhat_s[j] = rw[:, LANES:]
        m_s[j] = mg_t[:, :LANES].astype(BF16)
        g_s[j] = mg_t[:, LANES:].T
        gtot_s[8 * j:8 * j + 1, :] = jnp.exp(cum_last)

    def chain_and_output():
        ht = ht_ref[...]
        for j in range(n_chunks):
            y_st = _dot_nt(rhat_s[j], ht) + yhat_s[j]
            y_s[j * c:(j + 1) * c, :] = y_st[:c] + y_st[c:]
            ht = ht * gtot_s[8 * j:8 * j + 1, :] + _dot_nt(ht, m_s[j]) + g_s[j]
            yield
        ht_ref[...] = ht
        y = y_s[...]
        mu = _dot_hi2(y, gmean)
        yield
        d = y - mu
        var = _dot_hi2(d * d, gmean)
        yield
        yn = d * lax.rsqrt(var + RWKV_GN_EPS) * lng_ref[...] + lnb_ref[...]
        o_ref[...] = ((yn + bonus_s[slot_done]) * gate_s[slot_done]).astype(o_ref.dtype)

    _lockstep([chunk_terms(j) for j in range(n_chunks)]
              + [chain_and_output(), _delayed(prepare(nr_ref, nk_ref, nv_ref, nl_ref, slot_next), 2)])


def _rwkv(z, params, batch, seq, col, tb=512):
    nt = seq // tb
    n_pairs = 4

    def zspec(name, block_of_step, width=LANES):
        off = col[name]
        if width == LANES:
            return pl.BlockSpec((tb, LANES), lambda b, p, t: (b * nt + block_of_step(t), off + p))
        return pl.BlockSpec((tb, width), lambda b, p, t: (b * nt + block_of_step(t), off * LANES // width))

    first_block = lambda t: 0
    next_block = lambda t: jnp.minimum(t + 1, nt - 1)

    def pair_row(x):
        return x.reshape(1, -1), pl.BlockSpec((1, LANES), lambda b, p, t: (0, p))

    def pair_cols(x):
        return x.astype(BF16), pl.BlockSpec((x.shape[0], LANES), lambda b, p, t: (0, p))

    def whole_row(x):
        return x.reshape(1, -1), pl.BlockSpec((1, x.size), lambda b, p, t: (0, 0))

    args, specs = [z] * 8, []
    for block_of_step in (first_block, next_block):
        specs += [zspec("rr", block_of_step), zspec("kr", block_of_step), zspec("vr", block_of_step),
                  zspec("lora", block_of_step, 2 * LANES)]
    for a, s in (pair_row(params["mu_r"]), pair_row(params["mu_k"]), pair_row(params["mu_v"]),
                 whole_row(params["mu_l"]),
                 pair_row(params["w0"]), pair_cols(params["w_up"]),
                 pair_row(params["a0"]), pair_cols(params["a_up"]), pair_cols(params["g_up"]),
                 pair_row(params["k_k"]), pair_row(params["k_a"]), pair_row(params["r_k"]),
                 pair_row(params["ln_g"]), pair_row(params["ln_b"])):
        args.append(a)
        specs.append(s)

    n_chunks = tb // RWKV_CHUNK
    blk = pltpu.VMEM((tb, LANES), F32)
    ring = pltpu.VMEM((3, tb, LANES), F32)
    mat = lambda dtype: pltpu.VMEM((n_chunks, LANES, LANES), dtype)
    return pl.pallas_call(
        functools.partial(_rwkv_kernel, tb=tb),
        grid=(batch, n_pairs, nt + 1),
        in_specs=specs,
        out_specs=pl.BlockSpec((tb, LANES), lambda b, p, t: (b * nt + jnp.maximum(t - 1, 0), p)),
        out_shape=jax.ShapeDtypeStruct((batch * seq, n_pairs * LANES), BF16),
        scratch_shapes=[pltpu.VMEM((LANES, LANES), F32),
                        pltpu.VMEM((8, LANES), F32),
                        pltpu.VMEM((8, 2 * LANES), F32),
                        blk, blk, blk, blk, blk, blk,
                        ring, ring,
                        mat(BF16), mat(F32), mat(BF16), mat(F32),
                        pltpu.VMEM((8 * n_chunks, LANES), F32),
                        blk],
        compiler_params=pltpu.CompilerParams(
            dimension_semantics=("parallel", "parallel", "arbitrary"),
            vmem_limit_bytes=VMEM_LIMIT),
        name="rwkv7",
    )(*args)


def _swa_kernel(sink_ref, q_ref, kv_ref, kvp_ref, o_ref, *, n_qblocks, layer_sink_base):
    n = pl.program_id(1)
    w = SWA_WINDOW
    m0 = _head_mask(w)
    from_prev = _iota2((PAIR * w, w), 1) > _iota2((PAIR * w, w), 0) % w
    prev_bias = jnp.where(n > 0, 0.0, -jnp.inf).astype(F32)
    second_head = _iota2((PAIR * w, 1), 0) >= w
    kv = kv_ref[...]
    kvp = kvp_ref[...]
    n_kv_cols = kv.shape[1] // 2

    def qblock(qb):
        h = qb // 2
        cols = slice(qb * LANES, (qb + 1) * LANES)
        kcols = slice(h * LANES, (h + 1) * LANES)
        vcols = slice(n_kv_cols + h * LANES, n_kv_cols + (h + 1) * LANES)
        q = q_ref[:, cols] * (HEAD_DIM ** -0.5)
        kd = jnp.concatenate([kvp[:, kcols], kv[:, kcols]], axis=0)
        vd = jnp.concatenate([kvp[:, vcols], kv[:, vcols]], axis=0)
        s = _dot_nt(_stack_heads(q, m0), kd)
        yield
        sm = jnp.where(from_prev, s[:, :w] + prev_bias, s[:, w:])
        sink = jnp.where(second_head, sink_ref[layer_sink_base + 2 * qb + 1],
                         sink_ref[layer_sink_base + 2 * qb])
        m = jnp.maximum(jnp.max(sm, axis=-1, keepdims=True), sink)
        e = jnp.exp(sm - m)
        denom = jnp.sum(e, axis=-1, keepdims=True) + jnp.exp(sink - m)
        e_split = jnp.concatenate([jnp.where(from_prev, e, 0.0).astype(BF16),
                                   jnp.where(from_prev, 0.0, e).astype(BF16)], axis=1)
        pv = _dot(e_split, vd)
        yield
        o_ref[:, cols] = _unstack_heads(pv / denom, m0).astype(o_ref.dtype)

    _lockstep(qblock(qb) for qb in range(n_qblocks))


def _swa(z, sinks_flat, layer, batch, seq, n_heads=16):
    w = SWA_WINDOW
    nb = seq // w
    qw = n_heads * HEAD_DIM
    kvw = z.shape[1] - qw
    assert kvw == qw, "duplicated k|v block must be as wide as the q block"
    return pl.pallas_call(
        functools.partial(_swa_kernel, n_qblocks=qw // LANES, layer_sink_base=layer * n_heads),
        grid=(batch, nb),
        in_specs=[pl.BlockSpec(memory_space=pltpu.SMEM),
                  pl.BlockSpec((w, qw), lambda b, n: (b * nb + n, 0)),
                  pl.BlockSpec((w, kvw), lambda b, n: (b * nb + n, 1)),
                  pl.BlockSpec((w, kvw), lambda b, n: (b * nb + jnp.maximum(n - 1, 0), 1))],
        out_specs=pl.BlockSpec((w, qw), lambda b, n: (b * nb + n, 0)),
        out_shape=jax.ShapeDtypeStruct((batch * seq, qw), BF16),
        compiler_params=pltpu.CompilerParams(dimension_semantics=("parallel", "arbitrary"),
                                             vmem_limit_bytes=VMEM_LIMIT),
        name="swa",
    )(sinks_flat, z, z, z)


def _post_kernel(*refs, n_mix, has_bias, has_final):
    it = iter(refs)
    h_ref = next(it)
    mix_refs = [next(it) for _ in range(n_mix)]
    wo_refs = [next(it) for _ in range(n_mix)]
    bo_ref = next(it) if has_bias else None
    g2_ref, wg_ref, wu_ref, wd_ref = next(it), next(it), next(it), next(it)
    gf_ref = next(it) if has_final else None
    o_ref = next(it)

    h = h_ref[...]
    for m_ref, w_ref in zip(mix_refs, wo_refs):
        h = h + jnp.dot(m_ref[...].astype(BF16), w_ref[...], preferred_element_type=F32)
    if has_bias:
        h = h + bo_ref[...]
    n = _rms_norm(h, g2_ref[...]).astype(BF16)
    gate = jnp.dot(n, wg_ref[...], preferred_element_type=F32)
    up = jnp.dot(n, wu_ref[...], preferred_element_type=F32)
    act = (gate * jax.nn.sigmoid(gate) * up).astype(BF16)
    h = h + jnp.dot(act, wd_ref[...], preferred_element_type=F32)
    if has_final:
        h = _rms_norm(h, gf_ref[...])
    o_ref[...] = h


def _post(h2, mixes, wos, bo, g2, wg, wu, wd, final_g, tm=512):
    m, d = h2.shape
    row = lambda width: pl.BlockSpec((tm, width), lambda i: (i, 0))
    args, specs = [h2], [row(d)]
    for mx in mixes:
        args.append(mx)
        specs.append(row(mx.shape[1]))
    for w in wos:
        args.append(w.astype(BF16))
        specs.append(_const_spec(w.shape))
    if bo is not None:
        args.append(bo.reshape(1, d))
        specs.append(_const_spec((1, d)))
    for a in (g2.reshape(1, d), wg.astype(BF16), wu.astype(BF16), wd.astype(BF16)):
        args.append(a)
        specs.append(_const_spec(a.shape))
    if final_g is not None:
        args.append(final_g.reshape(1, d))
        specs.append(_const_spec((1, d)))
    return pl.pallas_call(
        functools.partial(_post_kernel, n_mix=len(mixes), has_bias=bo is not None,
                          has_final=final_g is not None),
        grid=(m // tm,),
        in_specs=specs,
        out_specs=row(d),
        out_shape=jax.ShapeDtypeStruct((m, d), F32),
        compiler_params=pltpu.CompilerParams(dimension_semantics=("parallel",),
                                             vmem_limit_bytes=VMEM_LIMIT),
        name="post_ffn",
    )(*args)


def _even_layout(w_in, mu, w_up, a_up):
    rw = 512
    d = w_in.shape[0]

    def interleave_halves(w):
        w5 = w.reshape(d, rw // LANES, PAIR, 2, HEAD_DIM // 2)
        return w5.transpose(0, 1, 3, 2, 4).reshape(d, rw)

    w = jnp.concatenate([interleave_halves(w_in[:, :rw]), interleave_halves(w_in[:, rw:2 * rw]),
                         w_in[:, 2 * rw:]], axis=1)
    names = ["q", "k", "v", "g", "rr", "kr", "vr", "lora"]
    col = {nm: i * (rw // LANES) for i, nm in enumerate(names)}
    lora = 64
    zeros = jnp.zeros((lora, w_up.shape[1]), w_up.dtype)
    params = {
        "mu_r": mu[:rw], "mu_k": mu[rw:2 * rw], "mu_v": mu[2 * rw:3 * rw], "mu_l": mu[3 * rw:],
        "w_up": jnp.concatenate([w_up, zeros], axis=0),
        "a_up": jnp.concatenate([zeros, a_up], axis=0),
    }
    return w, col, params


def _swa_layout(w_qkv, b_qkv, n_heads=16, n_kv=4):
    qw = n_heads * HEAD_DIM
    kw = n_kv * HEAD_DIM

    def dup(x):
        lead = x.shape[:-1]
        x4 = x.reshape(lead + (n_kv, 1, HEAD_DIM))
        return jnp.broadcast_to(x4, lead + (n_kv, PAIR, HEAD_DIM)).reshape(lead + (n_kv * LANES,))

    w = jnp.concatenate([w_qkv[:, :qw], dup(w_qkv[:, qw:qw + kw]), dup(w_qkv[:, qw + kw:])], axis=1)
    b = jnp.concatenate([b_qkv[:qw], dup(b_qkv[qw:qw + kw]), dup(b_qkv[qw + kw:])], axis=0)
    return w, b


def _rope_tables(seq, dtype):
    half = HEAD_DIM // 2
    inv_freq = ROPE_BASE ** (-jnp.linspace(0.0, 1.0, half, dtype=dtype))
    ang = jnp.arange(seq, dtype=dtype)[:, None] * inv_freq[None, :]
    sin = jnp.sin(ang)
    cos = jnp.cos(ang)
    return (jnp.tile(cos, (1, LANES // half)),
            jnp.concatenate([-sin] * PAIR + [sin] * PAIR, axis=1))


def kernel(x, norm1_g, norm2_g, final_g, even_w_in, even_w_out, rwkv_mu, rwkv_w0, rwkv_w_up,
           rwkv_a0, rwkv_a_up, rwkv_g_up, rwkv_k_k, rwkv_k_a, rwkv_r_k, rwkv_ln_g, rwkv_ln_b,
           swa_w_qkv, swa_b_qkv, swa_sinks, swa_w_o, swa_b_o, ffn_w_gate, ffn_w_up, ffn_w_down):
    batch, seq, d = x.shape
    depth = norm1_g.shape[0]
    h = x.reshape(batch * seq, d)
    cos, sin = _rope_tables(seq, x.dtype)
    sinks_flat = swa_sinks.reshape(-1)
    ret_w = 512
    for layer in range(depth):
        i = layer // 2
        last = layer == depth - 1
        if layer % 2 == 0:
            w, col, params = _even_layout(even_w_in[i], rwkv_mu[i], rwkv_w_up[i], rwkv_a_up[i])
            params.update(w0=rwkv_w0[i], a0=rwkv_a0[i], g_up=rwkv_g_up[i], k_k=rwkv_k_k[i],
                          k_a=rwkv_k_a[i], r_k=rwkv_r_k[i].reshape(-1),
                          ln_g=rwkv_ln_g[i].reshape(-1), ln_b=rwkv_ln_b[i].reshape(-1))
            z = _norm_proj(h, norm1_g[layer], w, None, F32)
            ret_out = _retention(z, cos, sin, batch, seq, col)
            rwkv_out = _rwkv(z, params, batch, seq, col)
            mixes = [ret_out, rwkv_out]
            wos = [even_w_out[i][:ret_w], even_w_out[i][ret_w:]]
            bo = None
        else:
            w, b = _swa_layout(swa_w_qkv[i], swa_b_qkv[i])
            z = _norm_proj(h, norm1_g[layer], w, b, BF16)
            mixes = [_swa(z, sinks_flat, i, batch, seq)]
            wos = [swa_w_o[i]]
            bo = swa_b_o[i]
        h = _post(h, mixes, wos, bo, norm2_g[layer], ffn_w_gate[layer], ffn_w_up[layer],
                  ffn_w_down[layer], final_g if last else None)
    return h.reshape(batch, seq, d)
```

```python
import functools

import numpy as np
import jax
import jax.numpy as jnp
from jax import lax
from jax.experimental import pallas as pl
from jax.experimental.pallas import tpu as pltpu

F32 = jnp.float32
BF16 = jnp.bfloat16

LANES = 128
HEAD_DIM = 64
PAIR = LANES // HEAD_DIM
RMS_EPS = 1e-6
RET_GN_EPS = 1e-6
RWKV_GN_EPS = 64e-5
ROPE_BASE = 10000.0
RET_CHUNK = 128
RWKV_CHUNK = 64
SWA_WINDOW = 128
VMEM_LIMIT = 56 * 1024 * 1024


def _dot(a, b):
    return jnp.dot(a.astype(BF16), b.astype(BF16), preferred_element_type=F32)


def _dot_nt(a, b):
    return lax.dot_general(a.astype(BF16), b.astype(BF16), (((1,), (1,)), ((), ())),
                           preferred_element_type=F32)


def _split2(x):
    hi = x.astype(BF16)
    lo = (x - hi.astype(F32)).astype(BF16)
    return hi, lo


def _dot_hi2(x, m):
    hi, lo = _split2(x)
    return (jnp.dot(hi, m, preferred_element_type=F32)
            + jnp.dot(lo, m, preferred_element_type=F32))


def _dot_hi3_left(m, x):
    hi = x.astype(BF16)
    r1 = x - hi.astype(F32)
    mid = r1.astype(BF16)
    lo = (r1 - mid.astype(F32)).astype(BF16)
    return (jnp.dot(m, hi, preferred_element_type=F32)
            + jnp.dot(m, mid, preferred_element_type=F32)
            + jnp.dot(m, lo, preferred_element_type=F32))


def _iota2(shape, dim):
    return lax.broadcasted_iota(jnp.int32, shape, dim)


def _head_mask(rows):
    return _iota2((rows, LANES), 1) < HEAD_DIM


def _stack_heads(x, m0):
    zero = jnp.zeros_like(x)
    return jnp.concatenate([jnp.where(m0, x, zero), jnp.where(m0, zero, x)], axis=0)


def _unstack_heads(xs, m0):
    r = xs.shape[0] // 2
    return jnp.where(m0, xs[:r], xs[r:])


def _group_matrix(scale):
    r = _iota2((LANES, LANES), 0) // HEAD_DIM
    c = _iota2((LANES, LANES), 1) // HEAD_DIM
    return jnp.where(r == c, scale, 0.0).astype(BF16)


def _rms_norm(x, g):
    ms = jnp.mean(x * x, axis=-1, keepdims=True)
    return x * lax.rsqrt(ms + RMS_EPS) * g


def _norm_proj_kernel(*refs, has_bias):
    if has_bias:
        h_ref, g_ref, w_ref, b_ref, o_ref = refs
    else:
        h_ref, g_ref, w_ref, o_ref = refs
    n = _rms_norm(h_ref[...], g_ref[...])
    z = jnp.dot(n.astype(BF16), w_ref[...], preferred_element_type=F32)
    if has_bias:
        z = z + b_ref[...]
    o_ref[...] = z.astype(o_ref.dtype)


def _layer_spec(x, layer, rows=None, row_block=0):
    _, r, c = x.shape
    return pl.BlockSpec((None, rows or r, c), lambda *_: (layer, row_block, 0),
                        pipeline_mode=pl.Buffered(1))


def _norm_proj(h2, g_all, layer, w_all, sub, bias_all, out_dtype, tm=512):
    m, d = h2.shape
    n = w_all.shape[2]
    in_specs = [pl.BlockSpec((tm, d), lambda i: (i, 0)), _layer_spec(g_all, layer), _layer_spec(w_all, sub)]
    args = [h2, g_all, w_all]
    if bias_all is not None:
        in_specs.append(_layer_spec(bias_all, sub))
        args.append(bias_all)
    bias = bias_all
    return pl.pallas_call(
        functools.partial(_norm_proj_kernel, has_bias=bias is not None),
        grid=(m // tm,),
        in_specs=in_specs,
        out_specs=pl.BlockSpec((tm, n), lambda i: (i, 0)),
        out_shape=jax.ShapeDtypeStruct((m, n), out_dtype),
        compiler_params=pltpu.CompilerParams(dimension_semantics=("parallel",),
                                             vmem_limit_bytes=VMEM_LIMIT),
        name="norm_proj",
    )(*args)


def _retention_kernel(zq_ref, zk_ref, zv_ref, zg_ref, cos_ref, sin_ref,
                      dmask_ref, qdec_ref, kdec_ref, cdec_ref, o_ref, s_ref, *, n_chunks):
    @pl.when(pl.program_id(2) == 0)
    def _():
        s_ref[...] = jnp.zeros_like(s_ref)

    c = RET_CHUNK
    half = HEAD_DIM // 2
    m0 = _head_mask(c)
    m0_qk = (_iota2((c, LANES), 1) // half) % PAIR == 0
    gmean = _group_matrix(1.0 / HEAD_DIM)
    key_head = (_iota2((LANES, LANES), 0) // half) % PAIR
    value_head = _iota2((LANES, LANES), 1) // HEAD_DIM
    same_head = key_head == value_head
    dmask = dmask_ref[0]
    qdec = qdec_ref[0]
    kdec = kdec_ref[0]
    cdec = cdec_ref[0]

    def swap_halves(x):
        return pltpu.roll(x, HEAD_DIM, axis=1)

    def inner_terms(j):
        rows = slice(j * c, (j + 1) * c)
        cos = cos_ref[rows, :]
        sin = sin_ref[rows, :]
        zq = zq_ref[rows, :]
        zk = zk_ref[rows, :]
        q = zq * cos + swap_halves(zq) * sin
        k = (zk * cos + swap_halves(zk) * sin) * (HEAD_DIM ** -0.5)
        v = zv_ref[rows, :].astype(BF16)
        scores = _dot_nt(_stack_heads(q, m0_qk), k) * dmask
        kv = jnp.where(same_head, _dot((k * kdec).T, v), 0.0)
        yield
        o_inner = _unstack_heads(_dot(scores, v), m0)
        return (q * qdec).astype(BF16), o_inner, kv

    inner = _lockstep(inner_terms(j) for j in range(n_chunks))

    state = s_ref[...]
    states = []
    for _, _, kv in inner:
        states.append(state)
        state = cdec * state + kv
    s_ref[...] = state

    def finish(j):
        rows = slice(j * c, (j + 1) * c)
        q_dec, o_inner, _ = inner[j]
        o = o_inner + _dot(q_dec, states[j])
        yield
        mu = _dot_hi2(o, gmean)
        yield
        d = o - mu
        var = _dot_hi2(d * d, gmean)
        yield
        g = zg_ref[rows, :]
        out = d * lax.rsqrt(var + RET_GN_EPS) * (g * jax.nn.sigmoid(g))
        o_ref[rows, :] = out.astype(o_ref.dtype)

    _lockstep(finish(j) for j in range(n_chunks))


def _retention_tables(dtype):
    c = RET_CHUNK
    n_heads = 8
    h = jnp.arange(n_heads, dtype=dtype)
    log_gamma = jnp.log1p(-(2.0 ** (-5.0 - h)))
    idx = jnp.arange(c, dtype=dtype)
    rel = idx[:, None] - idx[None, :]
    inner = jnp.where(rel >= 0, jnp.exp(jnp.maximum(rel, 0.0)[None] * log_gamma[:, None, None]), 0.0)
    dmask = inner.reshape(n_heads // PAIR, PAIR * c, c)
    pair_lg = log_gamma.reshape(n_heads // PAIR, PAIR)
    lanes_lg = jnp.repeat(pair_lg, HEAD_DIM, axis=1)
    qk_lg = jnp.tile(jnp.repeat(pair_lg, HEAD_DIM // 2, axis=1), (1, 2))
    qdec = jnp.exp((idx + 1.0)[None, :, None] * qk_lg[:, None, :])
    kdec = jnp.exp((c - 1 - idx)[None, :, None] * qk_lg[:, None, :])
    cdec = jnp.broadcast_to(jnp.exp(c * lanes_lg)[:, None, :], (n_heads // PAIR, LANES, LANES))
    return dmask, qdec, kdec, cdec


def _retention(z, cos, sin, batch, seq, col, tb=2048):
    nt = seq // tb
    n_pairs = 4
    dmask, qdec, kdec, cdec = _retention_tables(z.dtype)

    def zspec(name):
        off = col[name]
        return pl.BlockSpec((tb, LANES), lambda b, p, t: (b * nt + t, off + p))

    tab = lambda shape: pl.BlockSpec((1,) + shape, lambda b, p, t: (p, 0, 0))
    rope = pl.BlockSpec((tb, LANES), lambda b, p, t: (t, 0))
    return pl.pallas_call(
        functools.partial(_retention_kernel, n_chunks=tb // RET_CHUNK),
        grid=(batch, n_pairs, nt),
        in_specs=[zspec("q"), zspec("k"), zspec("v"), zspec("g"),
                  rope, rope,
                  tab((PAIR * RET_CHUNK, RET_CHUNK)), tab((RET_CHUNK, LANES)),
                  tab((RET_CHUNK, LANES)), tab((LANES, LANES))],
        out_specs=pl.BlockSpec((tb, LANES), lambda b, p, t: (b * nt + t, p)),
        out_shape=jax.ShapeDtypeStruct((batch * seq, n_pairs * LANES), BF16),
        scratch_shapes=[pltpu.VMEM((LANES, LANES), F32)],
        compiler_params=pltpu.CompilerParams(
            dimension_semantics=("parallel", "parallel", "arbitrary"),
            vmem_limit_bytes=VMEM_LIMIT),
        name="retention",
    )(z, z, z, z, cos, sin, dmask, qdec, kdec, cdec)


def _lockstep(gens):
    gens = list(gens)
    results = [None] * len(gens)
    live = list(range(len(gens)))
    while live:
        still = []
        for i in live:
            try:
                next(gens[i])
                still.append(i)
            except StopIteration as stop:
                results[i] = stop.value
        live = still
    return results


def _inverse_masks():
    r = _iota2((LANES, LANES), 0)
    c = _iota2((LANES, LANES), 1)
    eye = jnp.where(r == c, 1.0, 0.0).astype(F32)
    diag8 = r // 8 == c // 8
    lower_left = [(r // (2 * k) == c // (2 * k)) & (r // k > c // k) for k in (8, 16, 32)]
    return eye, diag8, lower_left


def _unit_lower_inverse(n_mat, masks):
    eye, diag8, lower_left = masks
    p = jnp.where(diag8, n_mat, 0.0)
    t = eye + p
    p = _dot(p, p)
    yield
    tp = _dot(jnp.concatenate([t, p], axis=0), p)
    t = t + tp[:LANES]
    p = tp[LANES:]
    yield
    t = t + _dot(t, p)
    yield
    for mask in lower_left:
        lt = _dot(jnp.where(mask, n_mat, 0.0), t)
        yield
        t = t + _dot(t, lt)
        yield
    return t


def _delayed(gen, rounds):
    for _ in range(rounds):
        yield
    yield from gen


def _rwkv_kernel(fr_ref, fk_ref, fv_ref, fl_ref, nr_ref, nk_ref, nv_ref, nl_ref,
                 mur_ref, muk_ref, muv_ref, mul_ref,
                 w0_ref, wup_ref, a0_ref, aup_ref, gup_ref, kk_ref, ka_ref, rk_ref, lng_ref, lnb_ref,
                 o_ref,
                 ht_ref, prev_ref, prevl_ref, r_s, lw_s, k_s, v_s, a_s, b_s, bonus_s, gate_s,
                 rhat_s, yhat_s, m_s, g_s, gtot_s, y_s, *, tb):
    t_idx = pl.program_id(2)
    c = RWKV_CHUNK
    n_chunks = tb // c
    gsum = _group_matrix(1.0)
    gmean = _group_matrix(1.0 / HEAD_DIM)
    pr = 2 * c
    first_row = _iota2((pr, 1), 0) == 0

    def prepare(zr_ref, zk_ref, zv_ref, zl_ref, slot):
        def prepare_rows(j):
            rows = slice(j * pr, (j + 1) * pr)

            def mixed(z_ref, mu_ref, carry_ref, carry_row):
                z = z_ref[rows, :]
                before = carry_ref[carry_row:carry_row + 1, :] if j == 0 else z_ref[j * pr - 1:j * pr, :]
                z_prev = jnp.where(first_row, before, pltpu.roll(z, 1, axis=0))
                return z + mu_ref[...] * (z_prev - z)

            xr = mixed(zr_ref, mur_ref, prev_ref, 0)
            xk = mixed(zk_ref, muk_ref, prev_ref, 1)
            xv = mixed(zv_ref, muv_ref, prev_ref, 2)
            xl = mixed(zl_ref, mul_ref, prevl_ref, 0)
            x_wa = xl[:, :LANES]
            w_pre = _dot(jnp.tanh(x_wa), wup_ref[...])
            a_pre = _dot(x_wa, aup_ref[...])
            gate = _dot(jax.nn.sigmoid(xl[:, LANES:]), gup_ref[...])
            kkf = xk * kk_ref[...]
            norm2 = _dot_hi2(kkf * kkf, gsum)
            yield
            w_log = -jax.nn.softplus(-(w0_ref[...] + w_pre)) - 0.5
            a = jax.nn.sigmoid(a0_ref[...] + a_pre)
            kk = kkf / jnp.maximum(jnp.sqrt(norm2), 1e-12)
            k2 = xk * (1.0 + (a - 1.0) * ka_ref[...])
            rk_sum = _dot_hi2(xr * k2 * rk_ref[...], gsum)
            gate_s[slot, rows, :] = gate
            r_s[rows, :] = xr
            lw_s[rows, :] = -jnp.exp(w_log)
            k_s[rows, :] = k2
            v_s[rows, :] = xv
            a_s[rows, :] = -kk
            b_s[rows, :] = kk * a
            yield
            bonus_s[slot, rows, :] = rk_sum * xv

        waiting = [prepare_rows(j) for j in range(tb // pr)]
        running = []
        while waiting or running:
            if waiting:
                running.append(waiting.pop(0))
            running = [g for g in running if next(g, "done") != "done"]
            yield
        prev_ref[0:1, :] = zr_ref[tb - 1:tb, :]
        prev_ref[1:2, :] = zk_ref[tb - 1:tb, :]
        prev_ref[2:3, :] = zv_ref[tb - 1:tb, :]
        prevl_ref[0:1, :] = zl_ref[tb - 1:tb, :]

    @pl.when(t_idx == 0)
    def _():
        for ref in (ht_ref, prev_ref, prevl_ref, bonus_s, gate_s, rhat_s, yhat_s, m_s, g_s, gtot_s):
            ref[...] = jnp.zeros_like(ref)
        for _ in prepare(fr_ref, fk_ref, fv_ref, fl_ref, 0):
            pass

    slot_next = lax.rem(t_idx + 1, 3)
    slot_done = lax.rem(t_idx + 2, 3)

    m0 = _head_mask(c)
    ri = _iota2((LANES, LANES), 0)
    ci = _iota2((LANES, LANES), 1)
    same_head = ri // c == ci // c
    strict = same_head & (ri > ci)
    incl = same_head & (ri >= ci)
    tri = jnp.where(_iota2((c, c), 0) >= _iota2((c, c), 1), 1.0, 0.0).astype(BF16)

    inv_masks = _inverse_masks()

    def chunk_terms(j):
        rows = slice(j * c, (j + 1) * c)
        lw = lw_s[rows, :]
        cum = _dot_hi3_left(tri, lw)
        yield
        cum_last = cum[c - 1:c, :]
        g_inv = jnp.exp(-cum)
        g_rem = jnp.exp(cum_last - cum)
        a_st = _stack_heads(a_s[rows, :] * jnp.exp(cum - lw), m0)
        r_st = _stack_heads(r_s[rows, :] * jnp.exp(cum), m0)
        v_st = _stack_heads(v_s[rows, :], m0)
        bg_st = _stack_heads(b_s[rows, :] * g_rem, m0)
        kg_st = _stack_heads(k_s[rows, :] * g_rem, m0)
        b_t = (b_s[rows, :] * g_inv).astype(BF16)
        k_t = (k_s[rows, :] * g_inv).astype(BF16)
        b2 = jnp.concatenate([b_t, b_t], axis=0)
        k2_ = jnp.concatenate([k_t, k_t], axis=0)
        ar_bf = jnp.concatenate([a_st, r_st], axis=0).astype(BF16)
        v_bf = v_st.astype(BF16)
        ar_b = _dot_nt(ar_bf, b2)
        ar_k = _dot_nt(ar_bf, k2_)
        n_ab = jnp.where(strict, ar_b[:LANES], 0.0)
        n_ak = jnp.where(strict, ar_k[:LANES], 0.0)
        n_rb = jnp.where(incl, ar_b[LANES:], 0.0)
        n_rk = jnp.where(incl, ar_k[LANES:], 0.0)
        yield
        akv = _dot(n_ak, v_bf)
        bk_t = jnp.concatenate([bg_st, kg_st], axis=0).T.astype(BF16)
        t_inv = yield from _unit_lower_inverse(n_ab, inv_masks)
        wu = _dot(t_inv, jnp.concatenate([a_st, akv], axis=1))
        yield
        wu_v = jnp.concatenate(
            [wu.astype(BF16), jnp.concatenate([jnp.zeros_like(v_bf), v_bf], axis=1)], axis=0)
        lhs = jnp.concatenate([jnp.concatenate([n_rb, n_rk], axis=1).astype(BF16), bk_t], axis=0)
        both = jnp.dot(lhs, wu_v, preferred_element_type=F32)
        rw = both[:LANES]
        mg_t = both[LANES:]
        yield
        rhat_s[j] = (r_st + rw[:, :LANES]).astype(BF16)
        yhat_s[j] = rw[:, LANES:]
        m_s[j] = mg_t[:, :LANES].astype(BF16)
        g_s[j] = mg_t[:, LANES:].T
        gtot_s[8 * j:8 * j + 1, :] = jnp.exp(cum_last)

    def chain_and_output():
        ht = ht_ref[...]
        for j in range(n_chunks):
            y_st = _dot_nt(rhat_s[j], ht) + yhat_s[j]
            y_s[j * c:(j + 1) * c, :] = y_st[:c] + y_st[c:]
            ht = ht * gtot_s[8 * j:8 * j + 1, :] + _dot_nt(ht, m_s[j]) + g_s[j]
            yield
        ht_ref[...] = ht
        y = y_s[...]
        mu = _dot_hi2(y, gmean)
        yield
        d = y - mu
        var = _dot_hi2(d * d, gmean)
        yield
        yn = d * lax.rsqrt(var + RWKV_GN_EPS) * lng_ref[...] + lnb_ref[...]
        o_ref[...] = ((yn + bonus_s[slot_done]) * gate_s[slot_done]).astype(o_ref.dtype)

    _lockstep([chunk_terms(j) for j in range(n_chunks)]
              + [chain_and_output(), _delayed(prepare(nr_ref, nk_ref, nv_ref, nl_ref, slot_next), 2)])


def _rwkv(z, params, layer, batch, seq, col, tb=512):
    nt = seq // tb
    n_pairs = 4

    def zspec(name, block_of_step, width=LANES):
        off = col[name]
        if width == LANES:
            return pl.BlockSpec((tb, LANES), lambda b, p, t: (b * nt + block_of_step(t), off + p))
        return pl.BlockSpec((tb, width), lambda b, p, t: (b * nt + block_of_step(t), off * LANES // width))

    first_block = lambda t: 0
    next_block = lambda t: jnp.minimum(t + 1, nt - 1)

    def pair_row(x, first_block=0):
        return x, pl.BlockSpec((None, 1, LANES), lambda b, p, t: (layer, 0, first_block + p))

    def pair_cols(x):
        return x, pl.BlockSpec((None, x.shape[1], LANES), lambda b, p, t: (layer, 0, p))

    mu = params["mu"]
    rw_blocks = 512 // LANES
    mu_lora = (mu, pl.BlockSpec((None, 1, 2 * LANES), lambda b, p, t: (layer, 0, 3 * rw_blocks // 2)))

    args, specs = [z] * 8, []
    for block_of_step in (first_block, next_block):
        specs += [zspec("rr", block_of_step), zspec("kr", block_of_step), zspec("vr", block_of_step),
                  zspec("lora", block_of_step, 2 * LANES)]
    for a, s in (pair_row(mu, 0), pair_row(mu, rw_blocks), pair_row(mu, 2 * rw_blocks), mu_lora,
                 pair_row(params["w0"]), pair_cols(params["w_up"]),
                 pair_row(params["a0"]), pair_cols(params["a_up"]), pair_cols(params["g_up"]),
                 pair_row(params["k_k"]), pair_row(params["k_a"]), pair_row(params["r_k"]),
                 pair_row(params["ln_g"]), pair_row(params["ln_b"])):
        args.append(a)
        specs.append(s)

    n_chunks = tb // RWKV_CHUNK
    blk = pltpu.VMEM((tb, LANES), F32)
    ring = pltpu.VMEM((3, tb, LANES), F32)
    mat = lambda dtype: pltpu.VMEM((n_chunks, LANES, LANES), dtype)
    return pl.pallas_call(
        functools.partial(_rwkv_kernel, tb=tb),
        grid=(batch, n_pairs, nt + 1),
        in_specs=specs,
        out_specs=pl.BlockSpec((tb, LANES), lambda b, p, t: (b * nt + jnp.maximum(t - 1, 0), p)),
        out_shape=jax.ShapeDtypeStruct((batch * seq, n_pairs * LANES), BF16),
        scratch_shapes=[pltpu.VMEM((LANES, LANES), F32),
                        pltpu.VMEM((8, LANES), F32),
                        pltpu.VMEM((8, 2 * LANES), F32),
                        blk, blk, blk, blk, blk, blk,
                        ring, ring,
                        mat(BF16), mat(F32), mat(BF16), mat(F32),
                        pltpu.VMEM((8 * n_chunks, LANES), F32),
                        blk],
        compiler_params=pltpu.CompilerParams(
            dimension_semantics=("parallel", "parallel", "arbitrary"),
            vmem_limit_bytes=VMEM_LIMIT),
        name="rwkv7",
    )(*args)


def _swa_kernel(sink_ref, q_ref, kv_ref, kvp_ref, o_ref, *, n_qblocks, layer_sink_base):
    n = pl.program_id(1)
    w = SWA_WINDOW
    m0 = _head_mask(w)
    from_prev = _iota2((PAIR * w, w), 1) > _iota2((PAIR * w, w), 0) % w
    prev_bias = jnp.where(n > 0, 0.0, -jnp.inf).astype(F32)
    second_head = _iota2((PAIR * w, 1), 0) >= w
    kv = kv_ref[...]
    kvp = kvp_ref[...]
    n_kv_cols = kv.shape[1] // 2

    def qblock(qb):
        h = qb // 2
        cols = slice(qb * LANES, (qb + 1) * LANES)
        kcols = slice(h * LANES, (h + 1) * LANES)
        vcols = slice(n_kv_cols + h * LANES, n_kv_cols + (h + 1) * LANES)
        q = q_ref[:, cols] * (HEAD_DIM ** -0.5)
        kd = jnp.concatenate([kvp[:, kcols], kv[:, kcols]], axis=0)
        vd = jnp.concatenate([kvp[:, vcols], kv[:, vcols]], axis=0)
        s = _dot_nt(_stack_heads(q, m0), kd)
        yield
        sm = jnp.where(from_prev, s[:, :w] + prev_bias, s[:, w:])
        sink = jnp.where(second_head, sink_ref[layer_sink_base + 2 * qb + 1],
                         sink_ref[layer_sink_base + 2 * qb])
        m = jnp.maximum(jnp.max(sm, axis=-1, keepdims=True), sink)
        e = jnp.exp(sm - m)
        denom = jnp.sum(e, axis=-1, keepdims=True) + jnp.exp(sink - m)
        e_split = jnp.concatenate([jnp.where(from_prev, e, 0.0).astype(BF16),
                                   jnp.where(from_prev, 0.0, e).astype(BF16)], axis=1)
        pv = _dot(e_split, vd)
        yield
        o_ref[:, cols] = _unstack_heads(pv / denom, m0).astype(o_ref.dtype)

    _lockstep(qblock(qb) for qb in range(n_qblocks))


def _swa(z, sinks_flat, layer, batch, seq, n_heads=16):
    w = SWA_WINDOW
    nb = seq // w
    qw = n_heads * HEAD_DIM
    kvw = z.shape[1] - qw
    assert kvw == qw, "duplicated k|v block must be as wide as the q block"
    return pl.pallas_call(
        functools.partial(_swa_kernel, n_qblocks=qw // LANES, layer_sink_base=layer * n_heads),
        grid=(batch, nb),
        in_specs=[pl.BlockSpec(memory_space=pltpu.SMEM),
                  pl.BlockSpec((w, qw), lambda b, n: (b * nb + n, 0)),
                  pl.BlockSpec((w, kvw), lambda b, n: (b * nb + n, 1)),
                  pl.BlockSpec((w, kvw), lambda b, n: (b * nb + jnp.maximum(n - 1, 0), 1))],
        out_specs=pl.BlockSpec((w, qw), lambda b, n: (b * nb + n, 0)),
        out_shape=jax.ShapeDtypeStruct((batch * seq, qw), BF16),
        compiler_params=pltpu.CompilerParams(dimension_semantics=("parallel", "arbitrary"),
                                             vmem_limit_bytes=VMEM_LIMIT),
        name="swa",
    )(sinks_flat, z, z, z)


def _post_kernel(*refs, n_mix, has_bias, has_final):
    it = iter(refs)
    h_ref = next(it)
    mix_refs = [next(it) for _ in range(n_mix)]
    wo_refs = [next(it) for _ in range(n_mix)]
    bo_ref = next(it) if has_bias else None
    g2_ref, wg_ref, wu_ref, wd_ref = next(it), next(it), next(it), next(it)
    gf_ref = next(it) if has_final else None
    o_ref = next(it)

    h = h_ref[...]
    for m_ref, w_ref in zip(mix_refs, wo_refs):
        h = h + jnp.dot(m_ref[...].astype(BF16), w_ref[...], preferred_element_type=F32)
    if has_bias:
        h = h + bo_ref[...]
    n = _rms_norm(h, g2_ref[...]).astype(BF16)
    gate = jnp.dot(n, wg_ref[...], preferred_element_type=F32)
    up = jnp.dot(n, wu_ref[...], preferred_element_type=F32)
    act = (gate * jax.nn.sigmoid(gate) * up).astype(BF16)
    h = h + jnp.dot(act, wd_ref[...], preferred_element_type=F32)
    if has_final:
        h = _rms_norm(h, gf_ref[...])
    o_ref[...] = h


def _post(h2, mixes, wo_all, sub, bo_all, g2_all, wg_all, wu_all, wd_all, layer, final_g, tm=512):
    m, d = h2.shape
    row = lambda width: pl.BlockSpec((tm, width), lambda i: (i, 0))
    args, specs = [h2], [row(d)]
    for mx in mixes:
        args.append(mx)
        specs.append(row(mx.shape[1]))
    for k, mx in enumerate(mixes):
        args.append(wo_all)
        specs.append(_layer_spec(wo_all, sub, rows=mx.shape[1], row_block=k))
    if bo_all is not None:
        args.append(bo_all)
        specs.append(_layer_spec(bo_all, sub))
    for a in (g2_all, wg_all, wu_all, wd_all):
        args.append(a)
        specs.append(_layer_spec(a, layer))
    if final_g is not None:
        args.append(final_g.reshape(1, d))
        specs.append(pl.BlockSpec((1, d), lambda i: (0, 0), pipeline_mode=pl.Buffered(1)))
    bo = bo_all
    return pl.pallas_call(
        functools.partial(_post_kernel, n_mix=len(mixes), has_bias=bo is not None,
                          has_final=final_g is not None),
        grid=(m // tm,),
        in_specs=specs,
        out_specs=row(d),
        out_shape=jax.ShapeDtypeStruct((m, d), F32),
        compiler_params=pltpu.CompilerParams(dimension_semantics=("parallel",),
                                             vmem_limit_bytes=VMEM_LIMIT),
        name="post_ffn",
    )(*args)


def _even_layout(w_in, w_up, a_up):
    n_layers, d, _ = w_in.shape
    rw = 512

    def interleave_halves(w):
        w6 = w.reshape(n_layers, d, rw // LANES, PAIR, 2, HEAD_DIM // 2)
        return w6.transpose(0, 1, 2, 4, 3, 5).reshape(n_layers, d, rw)

    w = jnp.concatenate([interleave_halves(w_in[:, :, :rw]), interleave_halves(w_in[:, :, rw:2 * rw]),
                         w_in[:, :, 2 * rw:]], axis=2).astype(BF16)
    names = ["q", "k", "v", "g", "rr", "kr", "vr", "lora"]
    col = {nm: i * (rw // LANES) for i, nm in enumerate(names)}
    zeros = jnp.zeros_like(w_up)
    w_up_pad = jnp.concatenate([w_up, zeros], axis=1).astype(BF16)
    a_up_pad = jnp.concatenate([zeros, a_up], axis=1).astype(BF16)
    return w, col, w_up_pad, a_up_pad


def _swa_layout(w_qkv, b_qkv, n_heads=16, n_kv=4):
    qw = n_heads * HEAD_DIM
    kw = n_kv * HEAD_DIM

    def dup(x):
        lead = x.shape[:-1]
        x4 = x.reshape(lead + (n_kv, 1, HEAD_DIM))
        return jnp.broadcast_to(x4, lead + (n_kv, PAIR, HEAD_DIM)).reshape(lead + (n_kv * LANES,))

    w = jnp.concatenate([w_qkv[..., :qw], dup(w_qkv[..., qw:qw + kw]), dup(w_qkv[..., qw + kw:])], axis=-1)
    b = jnp.concatenate([b_qkv[..., :qw], dup(b_qkv[..., qw:qw + kw]), dup(b_qkv[..., qw + kw:])], axis=-1)
    return w.astype(BF16), b[:, None, :]


def _rope_tables(seq, dtype):
    half = HEAD_DIM // 2
    inv_freq = ROPE_BASE ** (-jnp.linspace(0.0, 1.0, half, dtype=dtype))
    ang = jnp.arange(seq, dtype=dtype)[:, None] * inv_freq[None, :]
    sin = jnp.sin(ang)
    cos = jnp.cos(ang)
    return (jnp.tile(cos, (1, LANES // half)),
            jnp.concatenate([-sin] * PAIR + [sin] * PAIR, axis=1))


def kernel(x, norm1_g, norm2_g, final_g, even_w_in, even_w_out, rwkv_mu, rwkv_w0, rwkv_w_up,
           rwkv_a0, rwkv_a_up, rwkv_g_up, rwkv_k_k, rwkv_k_a, rwkv_r_k, rwkv_ln_g, rwkv_ln_b,
           swa_w_qkv, swa_b_qkv, swa_sinks, swa_w_o, swa_b_o, ffn_w_gate, ffn_w_up, ffn_w_down):
    batch, seq, d = x.shape
    depth = norm1_g.shape[0]
    h = x.reshape(batch * seq, d)
    cos, sin = _rope_tables(seq, x.dtype)
    sinks_flat = swa_sinks.reshape(-1)

    def rows(p):
        return p.reshape(p.shape[0], 1, -1)

    even_w, col, w_up_pad, a_up_pad = _even_layout(even_w_in, rwkv_w_up, rwkv_a_up)
    rwkv_params = dict(mu=rows(rwkv_mu), w0=rows(rwkv_w0), w_up=w_up_pad, a0=rows(rwkv_a0), a_up=a_up_pad,
                       g_up=rwkv_g_up.astype(BF16), k_k=rows(rwkv_k_k), k_a=rows(rwkv_k_a),
                       r_k=rows(rwkv_r_k), ln_g=rows(rwkv_ln_g), ln_b=rows(rwkv_ln_b))
    swa_w, swa_b = _swa_layout(swa_w_qkv, swa_b_qkv)
    g1, g2 = rows(norm1_g), rows(norm2_g)
    even_wo, swa_wo, swa_bo = even_w_out.astype(BF16), swa_w_o.astype(BF16), rows(swa_b_o)
    wg, wu, wd = ffn_w_gate.astype(BF16), ffn_w_up.astype(BF16), ffn_w_down.astype(BF16)

    for layer in range(depth):
        i = layer // 2
        if layer % 2 == 0:
            z = _norm_proj(h, g1, layer, even_w, i, None, F32)
            mixes = [_retention(z, cos, sin, batch, seq, col), _rwkv(z, rwkv_params, i, batch, seq, col)]
            wo, bo = even_wo, None
        else:
            z = _norm_proj(h, g1, layer, swa_w, i, swa_b, BF16)
            mixes = [_swa(z, sinks_flat, i, batch, seq)]
            wo, bo = swa_wo, swa_bo
        h = _post(h, mixes, wo, i, bo, g2, wg, wu, wd, layer, final_g if layer == depth - 1 else None)
    return h.reshape(batch, seq, d)
```

```python
import functools

import numpy as np
import jax
import jax.numpy as jnp
from jax import lax
from jax.experimental import pallas as pl
from jax.experimental.pallas import tpu as pltpu

F32 = jnp.float32
BF16 = jnp.bfloat16

LANES = 128
HEAD_DIM = 64
PAIR = LANES // HEAD_DIM
RMS_EPS = 1e-6
RET_GN_EPS = 1e-6
RWKV_GN_EPS = 64e-5
ROPE_BASE = 10000.0
RET_CHUNK = 128
RWKV_CHUNK = 64
SWA_WINDOW = 128
VMEM_LIMIT = 56 * 1024 * 1024


def _dot(a, b):
    return jnp.dot(a.astype(BF16), b.astype(BF16), preferred_element_type=F32)


def _dot_nt(a, b):
    return lax.dot_general(a.astype(BF16), b.astype(BF16), (((1,), (1,)), ((), ())),
                           preferred_element_type=F32)


def _split2(x):
    hi = x.astype(BF16)
    lo = (x - hi.astype(F32)).astype(BF16)
    return hi, lo


def _dot_hi2(x, m):
    hi, lo = _split2(x)
    return (jnp.dot(hi, m, preferred_element_type=F32)
            + jnp.dot(lo, m, preferred_element_type=F32))


def _dot_hi3_left(m, x):
    hi = x.astype(BF16)
    r1 = x - hi.astype(F32)
    mid = r1.astype(BF16)
    lo = (r1 - mid.astype(F32)).astype(BF16)
    return (jnp.dot(m, hi, preferred_element_type=F32)
            + jnp.dot(m, mid, preferred_element_type=F32)
            + jnp.dot(m, lo, preferred_element_type=F32))


def _iota2(shape, dim):
    return lax.broadcasted_iota(jnp.int32, shape, dim)


def _head_mask(rows):
    return _iota2((rows, LANES), 1) < HEAD_DIM


def _stack_heads(x, m0):
    zero = jnp.zeros_like(x)
    return jnp.concatenate([jnp.where(m0, x, zero), jnp.where(m0, zero, x)], axis=0)


def _unstack_heads(xs, m0):
    r = xs.shape[0] // 2
    return jnp.where(m0, xs[:r], xs[r:])


def _group_matrix(scale):
    r = _iota2((LANES, LANES), 0) // HEAD_DIM
    c = _iota2((LANES, LANES), 1) // HEAD_DIM
    return jnp.where(r == c, scale, 0.0).astype(BF16)


def _rms_norm(x, g):
    ms = jnp.mean(x * x, axis=-1, keepdims=True)
    return x * lax.rsqrt(ms + RMS_EPS) * g


def _norm_proj_kernel(*refs, has_bias):
    if has_bias:
        h_ref, g_ref, w_ref, b_ref, o_ref = refs
    else:
        h_ref, g_ref, w_ref, o_ref = refs
    n = _rms_norm(h_ref[...], g_ref[...])
    z = jnp.dot(n.astype(BF16), w_ref[...], preferred_element_type=F32)
    if has_bias:
        z = z + b_ref[...]
    o_ref[...] = z.astype(o_ref.dtype)


def _layer_spec(x, layer, rows=None, row_block=0):
    _, r, c = x.shape
    return pl.BlockSpec((None, rows or r, c), lambda *_: (layer, row_block, 0),
                        pipeline_mode=pl.Buffered(1))


def _norm_proj(h2, g_all, layer, w_all, sub, bias_all, out_dtype, tm=1024):
    m, d = h2.shape
    n = w_all.shape[2]
    in_specs = [pl.BlockSpec((tm, d), lambda i: (i, 0)), _layer_spec(g_all, layer), _layer_spec(w_all, sub)]
    args = [h2, g_all, w_all]
    if bias_all is not None:
        in_specs.append(_layer_spec(bias_all, sub))
        args.append(bias_all)
    bias = bias_all
    return pl.pallas_call(
        functools.partial(_norm_proj_kernel, has_bias=bias is not None),
        grid=(m // tm,),
        in_specs=in_specs,
        out_specs=pl.BlockSpec((tm, n), lambda i: (i, 0)),
        out_shape=jax.ShapeDtypeStruct((m, n), out_dtype),
        compiler_params=pltpu.CompilerParams(dimension_semantics=("parallel",),
                                             vmem_limit_bytes=VMEM_LIMIT),
        name="norm_proj",
    )(*args)


def _retention_kernel(zq_ref, zk_ref, zv_ref, zg_ref, cos_ref, sin_ref,
                      dmask_ref, qdec_ref, kdec_ref, cdec_ref, o_ref, s_ref, *, n_chunks):
    @pl.when(pl.program_id(2) == 0)
    def _():
        s_ref[...] = jnp.zeros_like(s_ref)

    c = RET_CHUNK
    half = HEAD_DIM // 2
    m0 = _head_mask(c)
    m0_qk = (_iota2((c, LANES), 1) // half) % PAIR == 0
    gmean = _group_matrix(1.0 / HEAD_DIM)
    key_head = (_iota2((LANES, LANES), 0) // half) % PAIR
    value_head = _iota2((LANES, LANES), 1) // HEAD_DIM
    same_head = key_head == value_head
    dmask = dmask_ref[0]
    qdec = qdec_ref[0]
    kdec = kdec_ref[0]
    cdec = cdec_ref[0]

    def swap_halves(x):
        return pltpu.roll(x, HEAD_DIM, axis=1)

    def inner_terms(j):
        rows = slice(j * c, (j + 1) * c)
        cos = cos_ref[rows, :]
        sin = sin_ref[rows, :]
        zq = zq_ref[rows, :]
        zk = zk_ref[rows, :]
        q = zq * cos + swap_halves(zq) * sin
        k = (zk * cos + swap_halves(zk) * sin) * (HEAD_DIM ** -0.5)
        v = zv_ref[rows, :].astype(BF16)
        scores = _dot_nt(_stack_heads(q, m0_qk), k) * dmask
        kv = jnp.where(same_head, _dot((k * kdec).T, v), 0.0)
        yield
        o_inner = _unstack_heads(_dot(scores, v), m0)
        return (q * qdec).astype(BF16), o_inner, kv

    inner = _lockstep(inner_terms(j) for j in range(n_chunks))

    state = s_ref[...]
    states = []
    for _, _, kv in inner:
        states.append(state)
        state = cdec * state + kv
    s_ref[...] = state

    def finish(j):
        rows = slice(j * c, (j + 1) * c)
        q_dec, o_inner, _ = inner[j]
        o = o_inner + _dot(q_dec, states[j])
        yield
        mu = _dot_hi2(o, gmean)
        yield
        d = o - mu
        var = _dot_hi2(d * d, gmean)
        yield
        g = zg_ref[rows, :]
        out = d * lax.rsqrt(var + RET_GN_EPS) * (g * jax.nn.sigmoid(g))
        o_ref[rows, :] = out.astype(o_ref.dtype)

    _lockstep(finish(j) for j in range(n_chunks))


def _retention_tables(dtype):
    c = RET_CHUNK
    n_heads = 8
    h = jnp.arange(n_heads, dtype=dtype)
    log_gamma = jnp.log1p(-(2.0 ** (-5.0 - h)))
    idx = jnp.arange(c, dtype=dtype)
    rel = idx[:, None] - idx[None, :]
    inner = jnp.where(rel >= 0, jnp.exp(jnp.maximum(rel, 0.0)[None] * log_gamma[:, None, None]), 0.0)
    dmask = inner.reshape(n_heads // PAIR, PAIR * c, c)
    pair_lg = log_gamma.reshape(n_heads // PAIR, PAIR)
    lanes_lg = jnp.repeat(pair_lg, HEAD_DIM, axis=1)
    qk_lg = jnp.tile(jnp.repeat(pair_lg, HEAD_DIM // 2, axis=1), (1, 2))
    qdec = jnp.exp((idx + 1.0)[None, :, None] * qk_lg[:, None, :])
    kdec = jnp.exp((c - 1 - idx)[None, :, None] * qk_lg[:, None, :])
    cdec = jnp.broadcast_to(jnp.exp(c * lanes_lg)[:, None, :], (n_heads // PAIR, LANES, LANES))
    return dmask, qdec, kdec, cdec


def _retention(z, cos, sin, batch, seq, col, tb=2048):
    nt = seq // tb
    n_pairs = 4
    dmask, qdec, kdec, cdec = _retention_tables(z.dtype)

    def zspec(name):
        off = col[name]
        return pl.BlockSpec((tb, LANES), lambda b, p, t: (b * nt + t, off + p))

    tab = lambda shape: pl.BlockSpec((1,) + shape, lambda b, p, t: (p, 0, 0))
    rope = pl.BlockSpec((tb, LANES), lambda b, p, t: (t, 0))
    return pl.pallas_call(
        functools.partial(_retention_kernel, n_chunks=tb // RET_CHUNK),
        grid=(batch, n_pairs, nt),
        in_specs=[zspec("q"), zspec("k"), zspec("v"), zspec("g"),
                  rope, rope,
                  tab((PAIR * RET_CHUNK, RET_CHUNK)), tab((RET_CHUNK, LANES)),
                  tab((RET_CHUNK, LANES)), tab((LANES, LANES))],
        out_specs=pl.BlockSpec((tb, LANES), lambda b, p, t: (b * nt + t, p)),
        out_shape=jax.ShapeDtypeStruct((batch * seq, n_pairs * LANES), BF16),
        scratch_shapes=[pltpu.VMEM((LANES, LANES), F32)],
        compiler_params=pltpu.CompilerParams(
            dimension_semantics=("parallel", "parallel", "arbitrary"),
            vmem_limit_bytes=VMEM_LIMIT),
        name="retention",
    )(z, z, z, z, cos, sin, dmask, qdec, kdec, cdec)


def _lockstep(gens):
    gens = list(gens)
    results = [None] * len(gens)
    live = list(range(len(gens)))
    while live:
        still = []
        for i in live:
            try:
                next(gens[i])
                still.append(i)
            except StopIteration as stop:
                results[i] = stop.value
        live = still
    return results


def _inverse_masks():
    r = _iota2((LANES, LANES), 0)
    c = _iota2((LANES, LANES), 1)
    eye = jnp.where(r == c, 1.0, 0.0).astype(F32)
    diag8 = r // 8 == c // 8
    lower_left = [(r // (2 * k) == c // (2 * k)) & (r // k > c // k) for k in (8, 16, 32)]
    return eye, diag8, lower_left


def _unit_lower_inverse(n_mat, masks):
    eye, diag8, lower_left = masks
    p = jnp.where(diag8, n_mat, 0.0)
    t = eye + p
    p = _dot(p, p)
    yield
    tp = _dot(jnp.concatenate([t, p], axis=0), p)
    t = t + tp[:LANES]
    p = tp[LANES:]
    yield
    t = t + _dot(t, p)
    yield
    for mask in lower_left:
        lt = _dot(jnp.where(mask, n_mat, 0.0), t)
        yield
        t = t + _dot(t, lt)
        yield
    return t


def _delayed(gen, rounds):
    for _ in range(rounds):
        yield
    yield from gen


def _rwkv_kernel(fr_ref, fk_ref, fv_ref, fl_ref, nr_ref, nk_ref, nv_ref, nl_ref,
                 mur_ref, muk_ref, muv_ref, mul_ref,
                 w0_ref, wup_ref, a0_ref, aup_ref, gup_ref, kk_ref, ka_ref, rk_ref, lng_ref, lnb_ref,
                 o_ref,
                 ht_ref, prev_ref, prevl_ref, r_s, lw_s, k_s, v_s, a_s, b_s, bonus_s, gate_s,
                 rhat_s, yhat_s, m_s, g_s, gtot_s, y_s, *, tb):
    t_idx = pl.program_id(2)
    c = RWKV_CHUNK
    n_chunks = tb // c
    gsum = _group_matrix(1.0)
    gmean = _group_matrix(1.0 / HEAD_DIM)
    pr = 2 * c
    first_row = _iota2((pr, 1), 0) == 0

    def prepare(zr_ref, zk_ref, zv_ref, zl_ref, slot):
        def prepare_rows(j):
            rows = slice(j * pr, (j + 1) * pr)

            def mixed(z_ref, mu_ref, carry_ref, carry_row):
                z = z_ref[rows, :]
                before = carry_ref[carry_row:carry_row + 1, :] if j == 0 else z_ref[j * pr - 1:j * pr, :]
                z_prev = jnp.where(first_row, before, pltpu.roll(z, 1, axis=0))
                return z + mu_ref[...] * (z_prev - z)

            xr = mixed(zr_ref, mur_ref, prev_ref, 0)
            xk = mixed(zk_ref, muk_ref, prev_ref, 1)
            xv = mixed(zv_ref, muv_ref, prev_ref, 2)
            xl = mixed(zl_ref, mul_ref, prevl_ref, 0)
            x_wa = xl[:, :LANES]
            w_pre = _dot(jnp.tanh(x_wa), wup_ref[...])
            a_pre = _dot(x_wa, aup_ref[...])
            gate = _dot(jax.nn.sigmoid(xl[:, LANES:]), gup_ref[...])
            kkf = xk * kk_ref[...]
            norm2 = _dot_hi2(kkf * kkf, gsum)
            yield
            w_log = -jax.nn.softplus(-(w0_ref[...] + w_pre)) - 0.5
            a = jax.nn.sigmoid(a0_ref[...] + a_pre)
            kk = kkf / jnp.maximum(jnp.sqrt(norm2), 1e-12)
            k2 = xk * (1.0 + (a - 1.0) * ka_ref[...])
            rk_sum = _dot_hi2(xr * k2 * rk_ref[...], gsum)
            gate_s[slot, rows, :] = gate
            r_s[rows, :] = xr
            lw_s[rows, :] = -jnp.exp(w_log)
            k_s[rows, :] = k2
            v_s[rows, :] = xv
            a_s[rows, :] = -kk
            b_s[rows, :] = kk * a
            yield
            bonus_s[slot, rows, :] = rk_sum * xv

        waiting = [prepare_rows(j) for j in range(tb // pr)]
        running = []
        while waiting or running:
            if waiting:
                running.append(waiting.pop(0))
            running = [g for g in running if next(g, "done") != "done"]
            yield
        prev_ref[0:1, :] = zr_ref[tb - 1:tb, :]
        prev_ref[1:2, :] = zk_ref[tb - 1:tb, :]
        prev_ref[2:3, :] = zv_ref[tb - 1:tb, :]
        prevl_ref[0:1, :] = zl_ref[tb - 1:tb, :]

    @pl.when(t_idx == 0)
    def _():
        for ref in (ht_ref, prev_ref, prevl_ref, bonus_s, gate_s, rhat_s, yhat_s, m_s, g_s, gtot_s):
            ref[...] = jnp.zeros_like(ref)
        for _ in prepare(fr_ref, fk_ref, fv_ref, fl_ref, 0):
            pass

    slot_next = lax.rem(t_idx + 1, 3)
    slot_done = lax.rem(t_idx + 2, 3)

    m0 = _head_mask(c)
    ri = _iota2((LANES, LANES), 0)
    ci = _iota2((LANES, LANES), 1)
    same_head = ri // c == ci // c
    strict = same_head & (ri > ci)
    incl = same_head & (ri >= ci)
    tri = jnp.where(_iota2((c, c), 0) >= _iota2((c, c), 1), 1.0, 0.0).astype(BF16)

    inv_masks = _inverse_masks()

    def chunk_terms(j):
        rows = slice(j * c, (j + 1) * c)
        lw = lw_s[rows, :]
        cum = _dot_hi3_left(tri, lw)
        yield
        cum_last = cum[c - 1:c, :]
        g_inv = jnp.exp(-cum)
        g_rem = jnp.exp(cum_last - cum)
        a_st = _stack_heads(a_s[rows, :] * jnp.exp(cum - lw), m0)
        r_st = _stack_heads(r_s[rows, :] * jnp.exp(cum), m0)
        v_st = _stack_heads(v_s[rows, :], m0)
        bg_st = _stack_heads(b_s[rows, :] * g_rem, m0)
        kg_st = _stack_heads(k_s[rows, :] * g_rem, m0)
        b_t = (b_s[rows, :] * g_inv).astype(BF16)
        k_t = (k_s[rows, :] * g_inv).astype(BF16)
        b2 = jnp.concatenate([b_t, b_t], axis=0)
        k2_ = jnp.concatenate([k_t, k_t], axis=0)
        ar_bf = jnp.concatenate([a_st, r_st], axis=0).astype(BF16)
        v_bf = v_st.astype(BF16)
        ar_b = _dot_nt(ar_bf, b2)
        ar_k = _dot_nt(ar_bf, k2_)
        n_ab = jnp.where(strict, ar_b[:LANES], 0.0)
        n_ak = jnp.where(strict, ar_k[:LANES], 0.0)
        n_rb = jnp.where(incl, ar_b[LANES:], 0.0)
        n_rk = jnp.where(incl, ar_k[LANES:], 0.0)
        yield
        akv = _dot(n_ak, v_bf)
        bk_t = jnp.concatenate([bg_st, kg_st], axis=0).T.astype(BF16)
        t_inv = yield from _unit_lower_inverse(n_ab, inv_masks)
        wu = _dot(t_inv, jnp.concatenate([a_st, akv], axis=1))
        yield
        wu_v = jnp.concatenate(
            [wu.astype(BF16), jnp.concatenate([jnp.zeros_like(v_bf), v_bf], axis=1)], axis=0)
        lhs = jnp.concatenate([jnp.concatenate([n_rb, n_rk], axis=1).astype(BF16), bk_t], axis=0)
        both = jnp.dot(lhs, wu_v, preferred_element_type=F32)
        rw = both[:LANES]
        mg_t = both[LANES:]
        yield
        rhat_s[j] = (r_st + rw[:, :LANES]).astype(BF16)
        yhat_s[j] = rw[:, LANES:]
        m_s[j] = mg_t[:, :LANES].astype(BF16)
        g_s[j] = mg_t[:, LANES:].T
        gtot_s[8 * j:8 * j + 1, :] = jnp.exp(cum_last)

    def chain_and_output():
        ht = ht_ref[...]
        for j in range(n_chunks):
            y_st = _dot_nt(rhat_s[j], ht) + yhat_s[j]
            y_s[j * c:(j + 1) * c, :] = y_st[:c] + y_st[c:]
            ht = ht * gtot_s[8 * j:8 * j + 1, :] + _dot_nt(ht, m_s[j]) + g_s[j]
            yield
        ht_ref[...] = ht
        y = y_s[...]
        mu = _dot_hi2(y, gmean)
        yield
        d = y - mu
        var = _dot_hi2(d * d, gmean)
        yield
        yn = d * lax.rsqrt(var + RWKV_GN_EPS) * lng_ref[...] + lnb_ref[...]
        o_ref[...] = ((yn + bonus_s[slot_done]) * gate_s[slot_done]).astype(o_ref.dtype)

    _lockstep([chunk_terms(j) for j in range(n_chunks)]
              + [chain_and_output(), _delayed(prepare(nr_ref, nk_ref, nv_ref, nl_ref, slot_next), 2)])


def _rwkv(z, params, layer, batch, seq, col, tb=512):
    nt = seq // tb
    n_pairs = 4

    def zspec(name, block_of_step, width=LANES):
        off = col[name]
        if width == LANES:
            return pl.BlockSpec((tb, LANES), lambda b, p, t: (b * nt + block_of_step(t), off + p))
        return pl.BlockSpec((tb, width), lambda b, p, t: (b * nt + block_of_step(t), off * LANES // width))

    first_block = lambda t: 0
    next_block = lambda t: jnp.minimum(t + 1, nt - 1)

    def pair_row(x, first_block=0):
        return x, pl.BlockSpec((None, 1, LANES), lambda b, p, t: (layer, 0, first_block + p))

    def pair_cols(x):
        return x, pl.BlockSpec((None, x.shape[1], LANES), lambda b, p, t: (layer, 0, p))

    mu = params["mu"]
    rw_blocks = 512 // LANES
    mu_lora = (mu, pl.BlockSpec((None, 1, 2 * LANES), lambda b, p, t: (layer, 0, 3 * rw_blocks // 2)))

    args, specs = [z] * 8, []
    for block_of_step in (first_block, next_block):
        specs += [zspec("rr", block_of_step), zspec("kr", block_of_step), zspec("vr", block_of_step),
                  zspec("lora", block_of_step, 2 * LANES)]
    for a, s in (pair_row(mu, 0), pair_row(mu, rw_blocks), pair_row(mu, 2 * rw_blocks), mu_lora,
                 pair_row(params["w0"]), pair_cols(params["w_up"]),
                 pair_row(params["a0"]), pair_cols(params["a_up"]), pair_cols(params["g_up"]),
                 pair_row(params["k_k"]), pair_row(params["k_a"]), pair_row(params["r_k"]),
                 pair_row(params["ln_g"]), pair_row(params["ln_b"])):
        args.append(a)
        specs.append(s)

    n_chunks = tb // RWKV_CHUNK
    blk = pltpu.VMEM((tb, LANES), F32)
    ring = pltpu.VMEM((3, tb, LANES), F32)
    mat = lambda dtype: pltpu.VMEM((n_chunks, LANES, LANES), dtype)
    return pl.pallas_call(
        functools.partial(_rwkv_kernel, tb=tb),
        grid=(batch, n_pairs, nt + 1),
        in_specs=specs,
        out_specs=pl.BlockSpec((tb, LANES), lambda b, p, t: (b * nt + jnp.maximum(t - 1, 0), p)),
        out_shape=jax.ShapeDtypeStruct((batch * seq, n_pairs * LANES), BF16),
        scratch_shapes=[pltpu.VMEM((LANES, LANES), F32),
                        pltpu.VMEM((8, LANES), F32),
                        pltpu.VMEM((8, 2 * LANES), F32),
                        blk, blk, blk, blk, blk, blk,
                        ring, ring,
                        mat(BF16), mat(F32), mat(BF16), mat(F32),
                        pltpu.VMEM((8 * n_chunks, LANES), F32),
                        blk],
        compiler_params=pltpu.CompilerParams(
            dimension_semantics=("parallel", "parallel", "arbitrary"),
            vmem_limit_bytes=VMEM_LIMIT),
        name="rwkv7",
    )(*args)


def _swa_kernel(sink_ref, q_ref, kv_ref, kvp_ref, o_ref, *, n_qblocks, n_windows, layer_sink_base):
    n = pl.program_id(1)
    w = SWA_WINDOW
    m0 = _head_mask(w)
    from_prev = _iota2((PAIR * w, w), 1) > _iota2((PAIR * w, w), 0) % w
    first_bias = jnp.where(n > 0, 0.0, -jnp.inf).astype(F32)
    second_head = _iota2((PAIR * w, 1), 0) >= w
    n_kv_cols = kv_ref.shape[1] // 2

    def qblock(win, qb):
        h = qb // 2
        rows = slice(win * w, (win + 1) * w)
        cols = slice(qb * LANES, (qb + 1) * LANES)
        kcols = slice(h * LANES, (h + 1) * LANES)
        vcols = slice(n_kv_cols + h * LANES, n_kv_cols + (h + 1) * LANES)
        prev_ref, prev_rows = (kvp_ref, slice(0, w)) if win == 0 else (kv_ref, slice((win - 1) * w, win * w))
        prev_bias = first_bias if win == 0 else 0.0
        q = q_ref[rows, cols] * (HEAD_DIM ** -0.5)
        kd = jnp.concatenate([prev_ref[prev_rows, kcols], kv_ref[rows, kcols]], axis=0)
        vd = jnp.concatenate([prev_ref[prev_rows, vcols], kv_ref[rows, vcols]], axis=0)
        s = _dot_nt(_stack_heads(q, m0), kd)
        yield
        sm = jnp.where(from_prev, s[:, :w] + prev_bias, s[:, w:])
        sink = jnp.where(second_head, sink_ref[layer_sink_base + 2 * qb + 1],
                         sink_ref[layer_sink_base + 2 * qb])
        m = jnp.maximum(jnp.max(sm, axis=-1, keepdims=True), sink)
        e = jnp.exp(sm - m)
        denom = jnp.sum(e, axis=-1, keepdims=True) + jnp.exp(sink - m)
        e_split = jnp.concatenate([jnp.where(from_prev, e, 0.0).astype(BF16),
                                   jnp.where(from_prev, 0.0, e).astype(BF16)], axis=1)
        pv = _dot(e_split, vd)
        yield
        o_ref[rows, cols] = _unstack_heads(pv / denom, m0).astype(o_ref.dtype)

    _lockstep(qblock(win, qb) for win in range(n_windows) for qb in range(n_qblocks))


def _swa(z, sinks_flat, layer, batch, seq, n_heads=16, n_windows=2):
    w = SWA_WINDOW
    tq = n_windows * w
    nb = seq // tq
    qw = n_heads * HEAD_DIM
    kvw = z.shape[1] - qw
    assert kvw == qw, "duplicated k|v block must be as wide as the q block"
    return pl.pallas_call(
        functools.partial(_swa_kernel, n_qblocks=qw // LANES, n_windows=n_windows,
                          layer_sink_base=layer * n_heads),
        grid=(batch, nb),
        in_specs=[pl.BlockSpec(memory_space=pltpu.SMEM),
                  pl.BlockSpec((tq, qw), lambda b, n: (b * nb + n, 0)),
                  pl.BlockSpec((tq, kvw), lambda b, n: (b * nb + n, 1)),
                  pl.BlockSpec((w, kvw), lambda b, n: (jnp.maximum((b * nb + n) * n_windows - 1, 0), 1))],
        out_specs=pl.BlockSpec((tq, qw), lambda b, n: (b * nb + n, 0)),
        out_shape=jax.ShapeDtypeStruct((batch * seq, qw), BF16),
        compiler_params=pltpu.CompilerParams(dimension_semantics=("parallel", "arbitrary"),
                                             vmem_limit_bytes=VMEM_LIMIT),
        name="swa",
    )(sinks_flat, z, z, z)


def _post_kernel(*refs, n_mix, has_bias, has_final):
    it = iter(refs)
    h_ref = next(it)
    mix_refs = [next(it) for _ in range(n_mix)]
    wo_refs = [next(it) for _ in range(n_mix)]
    bo_ref = next(it) if has_bias else None
    g2_ref, wg_ref, wu_ref, wd_ref = next(it), next(it), next(it), next(it)
    gf_ref = next(it) if has_final else None
    o_ref = next(it)

    h = h_ref[...]
    for m_ref, w_ref in zip(mix_refs, wo_refs):
        h = h + jnp.dot(m_ref[...].astype(BF16), w_ref[...], preferred_element_type=F32)
    if has_bias:
        h = h + bo_ref[...]
    n = _rms_norm(h, g2_ref[...]).astype(BF16)
    gate = jnp.dot(n, wg_ref[...], preferred_element_type=F32)
    up = jnp.dot(n, wu_ref[...], preferred_element_type=F32)
    act = (gate * jax.nn.sigmoid(gate) * up).astype(BF16)
    h = h + jnp.dot(act, wd_ref[...], preferred_element_type=F32)
    if has_final:
        h = _rms_norm(h, gf_ref[...])
    o_ref[...] = h


def _post(h2, mixes, wo_all, sub, bo_all, g2_all, wg_all, wu_all, wd_all, layer, final_g, tm=512):
    m, d = h2.shape
    row = lambda width: pl.BlockSpec((tm, width), lambda i: (i, 0))
    args, specs = [h2], [row(d)]
    for mx in mixes:
        args.append(mx)
        specs.append(row(mx.shape[1]))
    for k, mx in enumerate(mixes):
        args.append(wo_all)
        specs.append(_layer_spec(wo_all, sub, rows=mx.shape[1], row_block=k))
    if bo_all is not None:
        args.append(bo_all)
        specs.append(_layer_spec(bo_all, sub))
    for a in (g2_all, wg_all, wu_all, wd_all):
        args.append(a)
        specs.append(_layer_spec(a, layer))
    if final_g is not None:
        args.append(final_g.reshape(1, d))
        specs.append(pl.BlockSpec((1, d), lambda i: (0, 0), pipeline_mode=pl.Buffered(1)))
    bo = bo_all
    return pl.pallas_call(
        functools.partial(_post_kernel, n_mix=len(mixes), has_bias=bo is not None,
                          has_final=final_g is not None),
        grid=(m // tm,),
        in_specs=specs,
        out_specs=row(d),
        out_shape=jax.ShapeDtypeStruct((m, d), F32),
        compiler_params=pltpu.CompilerParams(dimension_semantics=("parallel",),
                                             vmem_limit_bytes=VMEM_LIMIT),
        name="post_ffn",
    )(*args)


def _even_layout(w_in, w_up, a_up):
    n_layers, d, _ = w_in.shape
    rw = 512

    def interleave_halves(w):
        w6 = w.reshape(n_layers, d, rw // LANES, PAIR, 2, HEAD_DIM // 2)
        return w6.transpose(0, 1, 2, 4, 3, 5).reshape(n_layers, d, rw)

    w = jnp.concatenate([interleave_halves(w_in[:, :, :rw]), interleave_halves(w_in[:, :, rw:2 * rw]),
                         w_in[:, :, 2 * rw:]], axis=2).astype(BF16)
    names = ["q", "k", "v", "g", "rr", "kr", "vr", "lora"]
    col = {nm: i * (rw // LANES) for i, nm in enumerate(names)}
    zeros = jnp.zeros_like(w_up)
    w_up_pad = jnp.concatenate([w_up, zeros], axis=1).astype(BF16)
    a_up_pad = jnp.concatenate([zeros, a_up], axis=1).astype(BF16)
    return w, col, w_up_pad, a_up_pad


def _swa_layout(w_qkv, b_qkv, n_heads=16, n_kv=4):
    qw = n_heads * HEAD_DIM
    kw = n_kv * HEAD_DIM

    def dup(x):
        lead = x.shape[:-1]
        x4 = x.reshape(lead + (n_kv, 1, HEAD_DIM))
        return jnp.broadcast_to(x4, lead + (n_kv, PAIR, HEAD_DIM)).reshape(lead + (n_kv * LANES,))

    w = jnp.concatenate([w_qkv[..., :qw], dup(w_qkv[..., qw:qw + kw]), dup(w_qkv[..., qw + kw:])], axis=-1)
    b = jnp.concatenate([b_qkv[..., :qw], dup(b_qkv[..., qw:qw + kw]), dup(b_qkv[..., qw + kw:])], axis=-1)
    return w.astype(BF16), b[:, None, :]


def _rope_tables(seq, dtype):
    half = HEAD_DIM // 2
    inv_freq = ROPE_BASE ** (-jnp.linspace(0.0, 1.0, half, dtype=dtype))
    ang = jnp.arange(seq, dtype=dtype)[:, None] * inv_freq[None, :]
    sin = jnp.sin(ang)
    cos = jnp.cos(ang)
    return (jnp.tile(cos, (1, LANES // half)),
            jnp.concatenate([-sin] * PAIR + [sin] * PAIR, axis=1))


def kernel(x, norm1_g, norm2_g, final_g, even_w_in, even_w_out, rwkv_mu, rwkv_w0, rwkv_w_up,
           rwkv_a0, rwkv_a_up, rwkv_g_up, rwkv_k_k, rwkv_k_a, rwkv_r_k, rwkv_ln_g, rwkv_ln_b,
           swa_w_qkv, swa_b_qkv, swa_sinks, swa_w_o, swa_b_o, ffn_w_gate, ffn_w_up, ffn_w_down):
    batch, seq, d = x.shape
    depth = norm1_g.shape[0]
    h = x.reshape(batch * seq, d)
    cos, sin = _rope_tables(seq, x.dtype)
    sinks_flat = swa_sinks.reshape(-1)

    def rows(p):
        return p.reshape(p.shape[0], 1, -1)

    even_w, col, w_up_pad, a_up_pad = _even_layout(even_w_in, rwkv_w_up, rwkv_a_up)
    rwkv_params = dict(mu=rows(rwkv_mu), w0=rows(rwkv_w0), w_up=w_up_pad, a0=rows(rwkv_a0), a_up=a_up_pad,
                       g_up=rwkv_g_up.astype(BF16), k_k=rows(rwkv_k_k), k_a=rows(rwkv_k_a),
                       r_k=rows(rwkv_r_k), ln_g=rows(rwkv_ln_g), ln_b=rows(rwkv_ln_b))
    swa_w, swa_b = _swa_layout(swa_w_qkv, swa_b_qkv)
    g1, g2 = rows(norm1_g), rows(norm2_g)
    even_wo, swa_wo, swa_bo = even_w_out.astype(BF16), swa_w_o.astype(BF16), rows(swa_b_o)
    wg, wu, wd = ffn_w_gate.astype(BF16), ffn_w_up.astype(BF16), ffn_w_down.astype(BF16)

    for layer in range(depth):
        i = layer // 2
        if layer % 2 == 0:
            z = _norm_proj(h, g1, layer, even_w, i, None, F32)
            mixes = [_retention(z, cos, sin, batch, seq, col), _rwkv(z, rwkv_params, i, batch, seq, col)]
            wo, bo = even_wo, None
        else:
            z = _norm_proj(h, g1, layer, swa_w, i, swa_b, BF16)
            mixes = [_swa(z, sinks_flat, i, batch, seq)]
            wo, bo = swa_wo, swa_bo
        h = _post(h, mixes, wo, i, bo, g2, wg, wu, wd, layer, final_g if layer == depth - 1 else None)
    return h.reshape(batch, seq, d)
```

```python
import functools

import numpy as np
import jax
import jax.numpy as jnp
from jax import lax
from jax.experimental import pallas as pl
from jax.experimental.pallas import tpu as pltpu

F32 = jnp.float32
BF16 = jnp.bfloat16

LANES = 128
HEAD_DIM = 64
PAIR = LANES // HEAD_DIM
RMS_EPS = 1e-6
RET_GN_EPS = 1e-6
RWKV_GN_EPS = 64e-5
ROPE_BASE = 10000.0
RET_CHUNK = 128
RWKV_CHUNK = 64
SWA_WINDOW = 128
VMEM_LIMIT = 56 * 1024 * 1024


def _dot(a, b):
    return jnp.dot(a.astype(BF16), b.astype(BF16), preferred_element_type=F32)


def _dot_nt(a, b):
    return lax.dot_general(a.astype(BF16), b.astype(BF16), (((1,), (1,)), ((), ())),
                           preferred_element_type=F32)


def _split2(x):
    hi = x.astype(BF16)
    lo = (x - hi.astype(F32)).astype(BF16)
    return hi, lo


def _dot_hi2(x, m):
    hi, lo = _split2(x)
    return (jnp.dot(hi, m, preferred_element_type=F32)
            + jnp.dot(lo, m, preferred_element_type=F32))


def _dot_hi3_left(m, x):
    hi = x.astype(BF16)
    r1 = x - hi.astype(F32)
    mid = r1.astype(BF16)
    lo = (r1 - mid.astype(F32)).astype(BF16)
    return (jnp.dot(m, hi, preferred_element_type=F32)
            + jnp.dot(m, mid, preferred_element_type=F32)
            + jnp.dot(m, lo, preferred_element_type=F32))


def _iota2(shape, dim):
    return lax.broadcasted_iota(jnp.int32, shape, dim)


def _head_mask(rows):
    return _iota2((rows, LANES), 1) < HEAD_DIM


def _stack_heads(x, m0):
    zero = jnp.zeros_like(x)
    return jnp.concatenate([jnp.where(m0, x, zero), jnp.where(m0, zero, x)], axis=0)


def _unstack_heads(xs, m0):
    r = xs.shape[0] // 2
    return jnp.where(m0, xs[:r], xs[r:])


def _group_matrix(scale):
    r = _iota2((LANES, LANES), 0) // HEAD_DIM
    c = _iota2((LANES, LANES), 1) // HEAD_DIM
    return jnp.where(r == c, scale, 0.0).astype(BF16)


def _rms_norm(x, g):
    ms = jnp.mean(x * x, axis=-1, keepdims=True)
    return x * lax.rsqrt(ms + RMS_EPS) * g


def _norm_proj_kernel(*refs, has_bias):
    if has_bias:
        h_ref, g_ref, w_ref, b_ref, o_ref = refs
    else:
        h_ref, g_ref, w_ref, o_ref = refs
    n = _rms_norm(h_ref[...], g_ref[...])
    z = jnp.dot(n.astype(BF16), w_ref[...], preferred_element_type=F32)
    if has_bias:
        z = z + b_ref[...]
    o_ref[...] = z.astype(o_ref.dtype)


def _layer_spec(x, layer, rows=None, row_block=0):
    _, r, c = x.shape
    return pl.BlockSpec((None, rows or r, c), lambda *_: (layer, row_block, 0),
                        pipeline_mode=pl.Buffered(1))


def _norm_proj(h2, g_all, layer, w_all, sub, bias_all, out_dtype, tm=1024):
    m, d = h2.shape
    n = w_all.shape[2]
    in_specs = [pl.BlockSpec((tm, d), lambda i: (i, 0)), _layer_spec(g_all, layer), _layer_spec(w_all, sub)]
    args = [h2, g_all, w_all]
    if bias_all is not None:
        in_specs.append(_layer_spec(bias_all, sub))
        args.append(bias_all)
    bias = bias_all
    return pl.pallas_call(
        functools.partial(_norm_proj_kernel, has_bias=bias is not None),
        grid=(m // tm,),
        in_specs=in_specs,
        out_specs=pl.BlockSpec((tm, n), lambda i: (i, 0)),
        out_shape=jax.ShapeDtypeStruct((m, n), out_dtype),
        compiler_params=pltpu.CompilerParams(dimension_semantics=("parallel",),
                                             vmem_limit_bytes=VMEM_LIMIT),
        name="norm_proj",
    )(*args)


def _proj_retention_kernel(h_ref, g_ref, w_ref, cos_ref, sin_ref, dmask_ref, qdec_ref, kdec_ref, cdec_ref,
                           z_ref, o_ref, s_ref, *, n_pairs, blocks_per_seq, rest_stage_cols):
    @pl.when(pl.program_id(0) % blocks_per_seq == 0)
    def _():
        s_ref[...] = jnp.zeros_like(s_ref)

    n = _rms_norm(h_ref[...], g_ref[...]).astype(BF16)
    tm = n.shape[0]
    c = RET_CHUNK
    n_chunks = tm // c
    half = HEAD_DIM // 2
    m0 = _head_mask(c)
    m0_qk = (_iota2((c, LANES), 1) // half) % PAIR == 0
    gmean = _group_matrix(1.0 / HEAD_DIM)
    key_head = (_iota2((LANES, LANES), 0) // half) % PAIR
    value_head = _iota2((LANES, LANES), 1) // HEAD_DIM
    same_head = key_head == value_head
    pair_cols = 4 * LANES

    def project(c0, c1):
        return jnp.dot(n, w_ref[:, c0:c1], preferred_element_type=F32)

    def project_pair(p):
        parts = []
        for i in range(2):
            both = project(p * pair_cols + 2 * i * LANES, p * pair_cols + 2 * (i + 1) * LANES)
            parts += [both[:, :LANES], both[:, LANES:]]
            yield
        return parts

    def project_rest():
        base = n_pairs * pair_cols
        for c0 in range(0, z_ref.shape[1], rest_stage_cols):
            z_ref[:, c0:c0 + rest_stage_cols] = project(base + c0, base + c0 + rest_stage_cols)
            yield

    def swap_halves(x):
        return pltpu.roll(x, HEAD_DIM, axis=1)

    def retention_pair(p, zq, zk, zv, zg):
        dmask, qdec, kdec, cdec = dmask_ref[p], qdec_ref[p], kdec_ref[p], cdec_ref[p]

        def inner_terms(j):
            rows = slice(j * c, (j + 1) * c)
            cos = cos_ref[rows, :]
            sin = sin_ref[rows, :]
            q = zq[rows] * cos + swap_halves(zq[rows]) * sin
            k = (zk[rows] * cos + swap_halves(zk[rows]) * sin) * (HEAD_DIM ** -0.5)
            v = zv[rows].astype(BF16)
            scores = _dot_nt(_stack_heads(q, m0_qk), k) * dmask
            kv = jnp.where(same_head, _dot((k * kdec).T, v), 0.0)
            yield
            o_inner = _unstack_heads(_dot(scores, v), m0)
            return (q * qdec).astype(BF16), o_inner, kv

        inner = yield from _lockstep_stages(inner_terms(j) for j in range(n_chunks))
        state = s_ref[p]
        states = []
        for _, _, kv in inner:
            states.append(state)
            state = cdec * state + kv
        s_ref[p] = state

        def finish(j):
            rows = slice(j * c, (j + 1) * c)
            q_dec, o_inner, _ = inner[j]
            o = o_inner + _dot(q_dec, states[j])
            yield
            mu = _dot_hi2(o, gmean)
            yield
            d = o - mu
            var = _dot_hi2(d * d, gmean)
            yield
            g = zg[rows]
            out = d * lax.rsqrt(var + RET_GN_EPS) * (g * jax.nn.sigmoid(g))
            o_ref[rows, p * LANES:(p + 1) * LANES] = out.astype(o_ref.dtype)

        yield from _lockstep_stages(finish(j) for j in range(n_chunks))

    projected = _lockstep([project_pair(0)])[0]
    for p in range(n_pairs):
        upcoming = project_pair(p + 1) if p + 1 < n_pairs else project_rest()
        projected = _lockstep([retention_pair(p, *projected), upcoming])[1]


def _retention_tables(dtype):
    c = RET_CHUNK
    n_heads = 8
    h = jnp.arange(n_heads, dtype=dtype)
    log_gamma = jnp.log1p(-(2.0 ** (-5.0 - h)))
    idx = jnp.arange(c, dtype=dtype)
    rel = idx[:, None] - idx[None, :]
    inner = jnp.where(rel >= 0, jnp.exp(jnp.maximum(rel, 0.0)[None] * log_gamma[:, None, None]), 0.0)
    dmask = inner.reshape(n_heads // PAIR, PAIR * c, c)
    pair_lg = log_gamma.reshape(n_heads // PAIR, PAIR)
    lanes_lg = jnp.repeat(pair_lg, HEAD_DIM, axis=1)
    qk_lg = jnp.tile(jnp.repeat(pair_lg, HEAD_DIM // 2, axis=1), (1, 2))
    qdec = jnp.exp((idx + 1.0)[None, :, None] * qk_lg[:, None, :])
    kdec = jnp.exp((c - 1 - idx)[None, :, None] * qk_lg[:, None, :])
    cdec = jnp.broadcast_to(jnp.exp(c * lanes_lg)[:, None, :], (n_heads // PAIR, LANES, LANES))
    return dmask, qdec, kdec, cdec


def _proj_retention(h2, g_all, layer, w_all, sub, cos, sin, seq, n_pairs=4, tm=1024):
    m, d = h2.shape
    n_rest = w_all.shape[2] - n_pairs * 4 * LANES
    blocks_per_seq = seq // tm
    tables = _retention_tables(h2.dtype)
    whole = lambda x: pl.BlockSpec(x.shape, lambda i: (0,) * x.ndim, pipeline_mode=pl.Buffered(1))
    rope = pl.BlockSpec((tm, LANES), lambda i: (i % blocks_per_seq, 0))
    return pl.pallas_call(
        functools.partial(_proj_retention_kernel, n_pairs=n_pairs, blocks_per_seq=blocks_per_seq,
                          rest_stage_cols=2 * LANES),
        grid=(m // tm,),
        in_specs=[pl.BlockSpec((tm, d), lambda i: (i, 0)), _layer_spec(g_all, layer), _layer_spec(w_all, sub),
                  rope, rope] + [whole(t) for t in tables],
        out_specs=[pl.BlockSpec((tm, n_rest), lambda i: (i, 0)),
                   pl.BlockSpec((tm, n_pairs * LANES), lambda i: (i, 0))],
        out_shape=[jax.ShapeDtypeStruct((m, n_rest), F32),
                   jax.ShapeDtypeStruct((m, n_pairs * LANES), BF16)],
        scratch_shapes=[pltpu.VMEM((n_pairs, LANES, LANES), F32)],
        compiler_params=pltpu.CompilerParams(dimension_semantics=("arbitrary",),
                                             vmem_limit_bytes=VMEM_LIMIT),
        name="proj_retention",
    )(h2, g_all, w_all, cos, sin, *tables)


def _lockstep_stages(gens):
    gens = list(gens)
    results = [None] * len(gens)
    live = list(range(len(gens)))
    while live:
        still = []
        for i in live:
            try:
                next(gens[i])
                still.append(i)
            except StopIteration as stop:
                results[i] = stop.value
        live = still
        if live:
            yield
    return results


def _lockstep(gens):
    stages = _lockstep_stages(gens)
    while True:
        try:
            next(stages)
        except StopIteration as stop:
            return stop.value


def _inverse_masks():
    r = _iota2((LANES, LANES), 0)
    c = _iota2((LANES, LANES), 1)
    eye = jnp.where(r == c, 1.0, 0.0).astype(F32)
    diag8 = r // 8 == c // 8
    lower_left = [(r // (2 * k) == c // (2 * k)) & (r // k > c // k) for k in (8, 16, 32)]
    return eye, diag8, lower_left


def _unit_lower_inverse(n_mat, masks):
    eye, diag8, lower_left = masks
    p = jnp.where(diag8, n_mat, 0.0)
    t = eye + p
    p = _dot(p, p)
    yield
    tp = _dot(jnp.concatenate([t, p], axis=0), p)
    t = t + tp[:LANES]
    p = tp[LANES:]
    yield
    t = t + _dot(t, p)
    yield
    for mask in lower_left:
        lt = _dot(jnp.where(mask, n_mat, 0.0), t)
        yield
        t = t + _dot(t, lt)
        yield
    return t


def _delayed(gen, rounds):
    for _ in range(rounds):
        yield
    yield from gen


def _rwkv_kernel(fr_ref, fk_ref, fv_ref, fl_ref, nr_ref, nk_ref, nv_ref, nl_ref,
                 mur_ref, muk_ref, muv_ref, mul_ref,
                 w0_ref, wup_ref, a0_ref, aup_ref, gup_ref, kk_ref, ka_ref, rk_ref, lng_ref, lnb_ref,
                 o_ref,
                 ht_ref, prev_ref, prevl_ref, r_s, lw_s, k_s, v_s, a_s, b_s, bonus_s, gate_s,
                 rhat_s, yhat_s, m_s, g_s, gtot_s, y_s, *, tb):
    t_idx = pl.program_id(2)
    c = RWKV_CHUNK
    n_chunks = tb // c
    gsum = _group_matrix(1.0)
    gmean = _group_matrix(1.0 / HEAD_DIM)
    pr = 2 * c
    first_row = _iota2((pr, 1), 0) == 0

    def prepare(zr_ref, zk_ref, zv_ref, zl_ref, slot):
        def prepare_rows(j):
            rows = slice(j * pr, (j + 1) * pr)

            def mixed(z_ref, mu_ref, carry_ref, carry_row):
                z = z_ref[rows, :]
                before = carry_ref[carry_row:carry_row + 1, :] if j == 0 else z_ref[j * pr - 1:j * pr, :]
                z_prev = jnp.where(first_row, before, pltpu.roll(z, 1, axis=0))
                return z + mu_ref[...] * (z_prev - z)

            xr = mixed(zr_ref, mur_ref, prev_ref, 0)
            xk = mixed(zk_ref, muk_ref, prev_ref, 1)
            xv = mixed(zv_ref, muv_ref, prev_ref, 2)
            xl = mixed(zl_ref, mul_ref, prevl_ref, 0)
            x_wa = xl[:, :LANES]
            w_pre = _dot(jnp.tanh(x_wa), wup_ref[...])
            a_pre = _dot(x_wa, aup_ref[...])
            gate = _dot(jax.nn.sigmoid(xl[:, LANES:]), gup_ref[...])
            kkf = xk * kk_ref[...]
            norm2 = _dot_hi2(kkf * kkf, gsum)
            yield
            w_log = -jax.nn.softplus(-(w0_ref[...] + w_pre)) - 0.5
            a = jax.nn.sigmoid(a0_ref[...] + a_pre)
            kk = kkf / jnp.maximum(jnp.sqrt(norm2), 1e-12)
            k2 = xk * (1.0 + (a - 1.0) * ka_ref[...])
            rk_sum = _dot_hi2(xr * k2 * rk_ref[...], gsum)
            gate_s[slot, rows, :] = gate
            r_s[rows, :] = xr
            lw_s[rows, :] = -jnp.exp(w_log)
            k_s[rows, :] = k2
            v_s[rows, :] = xv
            a_s[rows, :] = -kk
            b_s[rows, :] = kk * a
            yield
            bonus_s[slot, rows, :] = rk_sum * xv

        waiting = [prepare_rows(j) for j in range(tb // pr)]
        running = []
        while waiting or running:
            if waiting:
                running.append(waiting.pop(0))
            running = [g for g in running if next(g, "done") != "done"]
            yield
        prev_ref[0:1, :] = zr_ref[tb - 1:tb, :]
        prev_ref[1:2, :] = zk_ref[tb - 1:tb, :]
        prev_ref[2:3, :] = zv_ref[tb - 1:tb, :]
        prevl_ref[0:1, :] = zl_ref[tb - 1:tb, :]

    @pl.when(t_idx == 0)
    def _():
        for ref in (ht_ref, prev_ref, prevl_ref, bonus_s, gate_s, rhat_s, yhat_s, m_s, g_s, gtot_s):
            ref[...] = jnp.zeros_like(ref)
        for _ in prepare(fr_ref, fk_ref, fv_ref, fl_ref, 0):
            pass

    slot_next = lax.rem(t_idx + 1, 3)
    slot_done = lax.rem(t_idx + 2, 3)

    m0 = _head_mask(c)
    ri = _iota2((LANES, LANES), 0)
    ci = _iota2((LANES, LANES), 1)
    same_head = ri // c == ci // c
    strict = same_head & (ri > ci)
    incl = same_head & (ri >= ci)
    tri = jnp.where(_iota2((c, c), 0) >= _iota2((c, c), 1), 1.0, 0.0).astype(BF16)

    inv_masks = _inverse_masks()

    def chunk_terms(j):
        rows = slice(j * c, (j + 1) * c)
        lw = lw_s[rows, :]
        cum = _dot_hi3_left(tri, lw)
        yield
        cum_last = cum[c - 1:c, :]
        g_inv = jnp.exp(-cum)
        g_rem = jnp.exp(cum_last - cum)
        a_st = _stack_heads(a_s[rows, :] * jnp.exp(cum - lw), m0)
        r_st = _stack_heads(r_s[rows, :] * jnp.exp(cum), m0)
        v_st = _stack_heads(v_s[rows, :], m0)
        bg_st = _stack_heads(b_s[rows, :] * g_rem, m0)
        kg_st = _stack_heads(k_s[rows, :] * g_rem, m0)
        b_t = (b_s[rows, :] * g_inv).astype(BF16)
        k_t = (k_s[rows, :] * g_inv).astype(BF16)
        b2 = jnp.concatenate([b_t, b_t], axis=0)
        k2_ = jnp.concatenate([k_t, k_t], axis=0)
        ar_bf = jnp.concatenate([a_st, r_st], axis=0).astype(BF16)
        v_bf = v_st.astype(BF16)
        ar_b = _dot_nt(ar_bf, b2)
        ar_k = _dot_nt(ar_bf, k2_)
        n_ab = jnp.where(strict, ar_b[:LANES], 0.0)
        n_ak = jnp.where(strict, ar_k[:LANES], 0.0)
        n_rb = jnp.where(incl, ar_b[LANES:], 0.0)
        n_rk = jnp.where(incl, ar_k[LANES:], 0.0)
        yield
        akv = _dot(n_ak, v_bf)
        bk_t = jnp.concatenate([bg_st, kg_st], axis=0).T.astype(BF16)
        t_inv = yield from _unit_lower_inverse(n_ab, inv_masks)
        wu = _dot(t_inv, jnp.concatenate([a_st, akv], axis=1))
        yield
        wu_v = jnp.concatenate(
            [wu.astype(BF16), jnp.concatenate([jnp.zeros_like(v_bf), v_bf], axis=1)], axis=0)
        lhs = jnp.concatenate([jnp.concatenate([n_rb, n_rk], axis=1).astype(BF16), bk_t], axis=0)
        both = jnp.dot(lhs, wu_v, preferred_element_type=F32)
        rw = both[:LANES]
        mg_t = both[LANES:]
        yield
        rhat_s[j] = (r_st + rw[:, :LANES]).astype(BF16)
        yhat_s[j] = rw[:, LANES:]
        m_s[j] = mg_t[:, :LANES].astype(BF16)
        g_s[j] = mg_t[:, LANES:].T
        gtot_s[8 * j:8 * j + 1, :] = jnp.exp(cum_last)

    def chain_and_output():
        ht = ht_ref[...]
        for j in range(n_chunks):
            y_st = _dot_nt(rhat_s[j], ht) + yhat_s[j]
            y_s[j * c:(j + 1) * c, :] = y_st[:c] + y_st[c:]
            ht = ht * gtot_s[8 * j:8 * j + 1, :] + _dot_nt(ht, m_s[j]) + g_s[j]
            yield
        ht_ref[...] = ht
        y = y_s[...]
        mu = _dot_hi2(y, gmean)
        yield
        d = y - mu
        var = _dot_hi2(d * d, gmean)
        yield
        yn = d * lax.rsqrt(var + RWKV_GN_EPS) * lng_ref[...] + lnb_ref[...]
        o_ref[...] = ((yn + bonus_s[slot_done]) * gate_s[slot_done]).astype(o_ref.dtype)

    _lockstep([chunk_terms(j) for j in range(n_chunks)]
              + [chain_and_output(), _delayed(prepare(nr_ref, nk_ref, nv_ref, nl_ref, slot_next), 2)])


def _rwkv(z, params, layer, batch, seq, col, tb=512):
    nt = seq // tb
    n_pairs = 4

    def zspec(name, block_of_step, width=LANES):
        off = col[name]
        if width == LANES:
            return pl.BlockSpec((tb, LANES), lambda b, p, t: (b * nt + block_of_step(t), off + p))
        return pl.BlockSpec((tb, width), lambda b, p, t: (b * nt + block_of_step(t), off * LANES // width))

    first_block = lambda t: 0
    next_block = lambda t: jnp.minimum(t + 1, nt - 1)

    def pair_row(x, first_block=0):
        return x, pl.BlockSpec((None, 1, LANES), lambda b, p, t: (layer, 0, first_block + p))

    def pair_cols(x):
        return x, pl.BlockSpec((None, x.shape[1], LANES), lambda b, p, t: (layer, 0, p))

    mu = params["mu"]
    rw_blocks = 512 // LANES
    mu_lora = (mu, pl.BlockSpec((None, 1, 2 * LANES), lambda b, p, t: (layer, 0, 3 * rw_blocks // 2)))

    args, specs = [z] * 8, []
    for block_of_step in (first_block, next_block):
        specs += [zspec("rr", block_of_step), zspec("kr", block_of_step), zspec("vr", block_of_step),
                  zspec("lora", block_of_step, 2 * LANES)]
    for a, s in (pair_row(mu, 0), pair_row(mu, rw_blocks), pair_row(mu, 2 * rw_blocks), mu_lora,
                 pair_row(params["w0"]), pair_cols(params["w_up"]),
                 pair_row(params["a0"]), pair_cols(params["a_up"]), pair_cols(params["g_up"]),
                 pair_row(params["k_k"]), pair_row(params["k_a"]), pair_row(params["r_k"]),
                 pair_row(params["ln_g"]), pair_row(params["ln_b"])):
        args.append(a)
        specs.append(s)

    n_chunks = tb // RWKV_CHUNK
    blk = pltpu.VMEM((tb, LANES), F32)
    ring = pltpu.VMEM((3, tb, LANES), F32)
    mat = lambda dtype: pltpu.VMEM((n_chunks, LANES, LANES), dtype)
    return pl.pallas_call(
        functools.partial(_rwkv_kernel, tb=tb),
        grid=(batch, n_pairs, nt + 1),
        in_specs=specs,
        out_specs=pl.BlockSpec((tb, LANES), lambda b, p, t: (b * nt + jnp.maximum(t - 1, 0), p)),
        out_shape=jax.ShapeDtypeStruct((batch * seq, n_pairs * LANES), BF16),
        scratch_shapes=[pltpu.VMEM((LANES, LANES), F32),
                        pltpu.VMEM((8, LANES), F32),
                        pltpu.VMEM((8, 2 * LANES), F32),
                        blk, blk, blk, blk, blk, blk,
                        ring, ring,
                        mat(BF16), mat(F32), mat(BF16), mat(F32),
                        pltpu.VMEM((8 * n_chunks, LANES), F32),
                        blk],
        compiler_params=pltpu.CompilerParams(
            dimension_semantics=("parallel", "parallel", "arbitrary"),
            vmem_limit_bytes=VMEM_LIMIT),
        name="rwkv7",
    )(*args)


def _swa_kernel(sink_ref, q_ref, kv_ref, kvp_ref, o_ref, *, n_qblocks, n_windows, layer_sink_base):
    n = pl.program_id(1)
    w = SWA_WINDOW
    m0 = _head_mask(w)
    keep_head0 = jnp.where(m0, 1.0, 0.0).astype(q_ref.dtype)
    keep_head1 = jnp.where(m0, 0.0, 1.0).astype(q_ref.dtype)
    from_prev =_iota2((PAIR * w, w), 1) > _iota2((PAIR * w, w), 0) % w
    first_bias = jnp.where(n > 0, 0.0, -jnp.inf).astype(F32)
    second_head = _iota2((PAIR * w, 1), 0) >= w
    n_kv_cols = kv_ref.shape[1] // 2

    def qblock(win, qb):
        h = qb // 2
        rows = slice(win * w, (win + 1) * w)
        cols = slice(qb * LANES, (qb + 1) * LANES)
        kcols = slice(h * LANES, (h + 1) * LANES)
        vcols = slice(n_kv_cols + h * LANES, n_kv_cols + (h + 1) * LANES)
        prev_ref, prev_rows = (kvp_ref, slice(0, w)) if win == 0 else (kv_ref, slice((win - 1) * w, win * w))
        prev_bias = first_bias if win == 0 else 0.0
        q = q_ref[rows, cols] * (HEAD_DIM ** -0.5)
        kd = jnp.concatenate([prev_ref[prev_rows, kcols], kv_ref[rows, kcols]], axis=0)
        vd = jnp.concatenate([prev_ref[prev_rows, vcols], kv_ref[rows, vcols]], axis=0)
        q_stacked = jnp.concatenate([q * keep_head0, q * keep_head1], axis=0)
        s = _dot_nt(q_stacked, kd)
        yield
        sm = jnp.where(from_prev, s[:, :w] + prev_bias, s[:, w:])
        sink = jnp.where(second_head, sink_ref[layer_sink_base + 2 * qb + 1],
                         sink_ref[layer_sink_base + 2 * qb])
        m = jnp.maximum(jnp.max(sm, axis=-1, keepdims=True), sink)
        e = jnp.exp(sm - m)
        denom = jnp.sum(e, axis=-1, keepdims=True) + jnp.exp(sink - m)
        e_split = jnp.concatenate([jnp.where(from_prev, e, 0.0).astype(BF16),
                                   jnp.where(from_prev, 0.0, e).astype(BF16)], axis=1)
        pv = _dot(e_split, vd)
        yield
        o_ref[rows, cols] = _unstack_heads(pv / denom, m0).astype(o_ref.dtype)

    _lockstep(qblock(win, qb) for win in range(n_windows) for qb in range(n_qblocks))


def _swa(z, sinks_flat, layer, batch, seq, n_heads=16, n_windows=2):
    w = SWA_WINDOW
    tq = n_windows * w
    nb = seq // tq
    qw = n_heads * HEAD_DIM
    kvw = z.shape[1] - qw
    assert kvw == qw, "duplicated k|v block must be as wide as the q block"
    return pl.pallas_call(
        functools.partial(_swa_kernel, n_qblocks=qw // LANES, n_windows=n_windows,
                          layer_sink_base=layer * n_heads),
        grid=(batch, nb),
        in_specs=[pl.BlockSpec(memory_space=pltpu.SMEM),
                  pl.BlockSpec((tq, qw), lambda b, n: (b * nb + n, 0)),
                  pl.BlockSpec((tq, kvw), lambda b, n: (b * nb + n, 1)),
                  pl.BlockSpec((w, kvw), lambda b, n: (jnp.maximum((b * nb + n) * n_windows - 1, 0), 1))],
        out_specs=pl.BlockSpec((tq, qw), lambda b, n: (b * nb + n, 0)),
        out_shape=jax.ShapeDtypeStruct((batch * seq, qw), BF16),
        compiler_params=pltpu.CompilerParams(dimension_semantics=("parallel", "arbitrary"),
                                             vmem_limit_bytes=VMEM_LIMIT),
        name="swa",
    )(sinks_flat, z, z, z)


def _post_kernel(*refs, n_mix, has_bias, has_final):
    it = iter(refs)
    h_ref = next(it)
    mix_refs = [next(it) for _ in range(n_mix)]
    wo_refs = [next(it) for _ in range(n_mix)]
    bo_ref = next(it) if has_bias else None
    g2_ref, wg_ref, wu_ref, wd_ref = next(it), next(it), next(it), next(it)
    gf_ref = next(it) if has_final else None
    o_ref = next(it)

    h = h_ref[...]
    for m_ref, w_ref in zip(mix_refs, wo_refs):
        h = h + jnp.dot(m_ref[...].astype(BF16), w_ref[...], preferred_element_type=F32)
    if has_bias:
        h = h + bo_ref[...]
    n = _rms_norm(h, g2_ref[...]).astype(BF16)
    gate = jnp.dot(n, wg_ref[...], preferred_element_type=F32)
    up = jnp.dot(n, wu_ref[...], preferred_element_type=F32)
    act = (gate * jax.nn.sigmoid(gate) * up).astype(BF16)
    h = h + jnp.dot(act, wd_ref[...], preferred_element_type=F32)
    if has_final:
        h = _rms_norm(h, gf_ref[...])
    o_ref[...] = h


def _post(h2, mixes, wo_all, sub, bo_all, g2_all, wg_all, wu_all, wd_all, layer, final_g, tm=512):
    m, d = h2.shape
    row = lambda width: pl.BlockSpec((tm, width), lambda i: (i, 0))
    args, specs = [h2], [row(d)]
    for mx in mixes:
        args.append(mx)
        specs.append(row(mx.shape[1]))
    for k, mx in enumerate(mixes):
        args.append(wo_all)
        specs.append(_layer_spec(wo_all, sub, rows=mx.shape[1], row_block=k))
    if bo_all is not None:
        args.append(bo_all)
        specs.append(_layer_spec(bo_all, sub))
    for a in (g2_all, wg_all, wu_all, wd_all):
        args.append(a)
        specs.append(_layer_spec(a, layer))
    if final_g is not None:
        args.append(final_g.reshape(1, d))
        specs.append(pl.BlockSpec((1, d), lambda i: (0, 0), pipeline_mode=pl.Buffered(1)))
    bo = bo_all
    return pl.pallas_call(
        functools.partial(_post_kernel, n_mix=len(mixes), has_bias=bo is not None,
                          has_final=final_g is not None),
        grid=(m // tm,),
        in_specs=specs,
        out_specs=row(d),
        out_shape=jax.ShapeDtypeStruct((m, d), F32),
        compiler_params=pltpu.CompilerParams(dimension_semantics=("parallel",),
                                             vmem_limit_bytes=VMEM_LIMIT),
        name="post_ffn",
    )(*args)


def _even_layout(w_in, w_up, a_up):
    n_layers, d, _ = w_in.shape
    rw = 512

    def interleave_halves(w):
        w6 = w.reshape(n_layers, d, rw // LANES, PAIR, 2, HEAD_DIM // 2)
        return w6.transpose(0, 1, 2, 4, 3, 5).reshape(n_layers, d, rw)

    q, k = interleave_halves(w_in[:, :, :rw]), interleave_halves(w_in[:, :, rw:2 * rw])
    v, g = w_in[:, :, 2 * rw:3 * rw], w_in[:, :, 3 * rw:4 * rw]
    per_pair = [x[:, :, p * LANES:(p + 1) * LANES] for p in range(rw // LANES) for x in (q, k, v, g)]
    w = jnp.concatenate(per_pair + [w_in[:, :, 4 * rw:]], axis=2).astype(BF16)
    names = ["rr", "kr", "vr", "lora"]
    col = {nm: i * (rw // LANES) for i, nm in enumerate(names)}
    zeros = jnp.zeros_like(w_up)
    w_up_pad = jnp.concatenate([w_up, zeros], axis=1).astype(BF16)
    a_up_pad = jnp.concatenate([zeros, a_up], axis=1).astype(BF16)
    return w, col, w_up_pad, a_up_pad


def _swa_layout(w_qkv, b_qkv, n_heads=16, n_kv=4):
    qw = n_heads * HEAD_DIM
    kw = n_kv * HEAD_DIM

    def dup(x):
        lead = x.shape[:-1]
        x4 = x.reshape(lead + (n_kv, 1, HEAD_DIM))
        return jnp.broadcast_to(x4, lead + (n_kv, PAIR, HEAD_DIM)).reshape(lead + (n_kv * LANES,))

    w = jnp.concatenate([w_qkv[..., :qw], dup(w_qkv[..., qw:qw + kw]), dup(w_qkv[..., qw + kw:])], axis=-1)
    b = jnp.concatenate([b_qkv[..., :qw], dup(b_qkv[..., qw:qw + kw]), dup(b_qkv[..., qw + kw:])], axis=-1)
    return w.astype(BF16), b[:, None, :]


def _rope_tables(seq, dtype):
    half = HEAD_DIM // 2
    inv_freq = ROPE_BASE ** (-jnp.linspace(0.0, 1.0, half, dtype=dtype))
    ang = jnp.arange(seq, dtype=dtype)[:, None] * inv_freq[None, :]
    sin = jnp.sin(ang)
    cos = jnp.cos(ang)
    return (jnp.tile(cos, (1, LANES // half)),
            jnp.concatenate([-sin] * PAIR + [sin] * PAIR, axis=1))


def kernel(x, norm1_g, norm2_g, final_g, even_w_in, even_w_out, rwkv_mu, rwkv_w0, rwkv_w_up,
           rwkv_a0, rwkv_a_up, rwkv_g_up, rwkv_k_k, rwkv_k_a, rwkv_r_k, rwkv_ln_g, rwkv_ln_b,
           swa_w_qkv, swa_b_qkv, swa_sinks, swa_w_o, swa_b_o, ffn_w_gate, ffn_w_up, ffn_w_down):
    batch, seq, d = x.shape
    depth = norm1_g.shape[0]
    h = x.reshape(batch * seq, d)
    cos, sin = _rope_tables(seq, x.dtype)
    sinks_flat = swa_sinks.reshape(-1)

    def rows(p):
        return p.reshape(p.shape[0], 1, -1)

    even_w, col, w_up_pad, a_up_pad = _even_layout(even_w_in, rwkv_w_up, rwkv_a_up)
    rwkv_params = dict(mu=rows(rwkv_mu), w0=rows(rwkv_w0), w_up=w_up_pad, a0=rows(rwkv_a0), a_up=a_up_pad,
                       g_up=rwkv_g_up.astype(BF16), k_k=rows(rwkv_k_k), k_a=rows(rwkv_k_a),
                       r_k=rows(rwkv_r_k), ln_g=rows(rwkv_ln_g), ln_b=rows(rwkv_ln_b))
    swa_w, swa_b = _swa_layout(swa_w_qkv, swa_b_qkv)
    g1, g2 = rows(norm1_g), rows(norm2_g)
    even_wo, swa_wo, swa_bo = even_w_out.astype(BF16), swa_w_o.astype(BF16), rows(swa_b_o)
    wg, wu, wd = ffn_w_gate.astype(BF16), ffn_w_up.astype(BF16), ffn_w_down.astype(BF16)

    for layer in range(depth):
        i = layer // 2
        if layer % 2 == 0:
            z, ret_out = _proj_retention(h, g1, layer, even_w, i, cos, sin, seq)
            mixes = [ret_out, _rwkv(z, rwkv_params, i, batch, seq, col)]
            wo, bo = even_wo, None
        else:
            z = _norm_proj(h, g1, layer, swa_w, i, swa_b, BF16)
            mixes = [_swa(z, sinks_flat, i, batch, seq)]
            wo, bo = swa_wo, swa_bo
        h = _post(h, mixes, wo, i, bo, g2, wg, wu, wd, layer, final_g if layer == depth - 1 else None)
    return h.reshape(batch, seq, d)
```

```python
import functools

import numpy as np
import jax
import jax.numpy as jnp
from jax import lax
from jax.experimental import pallas as pl
from jax.experimental.pallas import tpu as pltpu

F32 = jnp.float32
BF16 = jnp.bfloat16

LANES = 128
HEAD_DIM = 64
PAIR = LANES // HEAD_DIM
RMS_EPS = 1e-6
RET_GN_EPS = 1e-6
RWKV_GN_EPS = 64e-5
ROPE_BASE = 10000.0
RET_CHUNK = 128
RWKV_CHUNK = 64
SWA_WINDOW = 128
VMEM_LIMIT = 56 * 1024 * 1024


def _dot(a, b):
    return jnp.dot(a.astype(BF16), b.astype(BF16), preferred_element_type=F32)


def _dot_nt(a, b):
    return lax.dot_general(a.astype(BF16), b.astype(BF16), (((1,), (1,)), ((), ())),
                           preferred_element_type=F32)


def _split2(x):
    hi = x.astype(BF16)
    lo = (x - hi.astype(F32)).astype(BF16)
    return hi, lo


def _dot_hi2(x, m):
    hi, lo = _split2(x)
    return (jnp.dot(hi, m, preferred_element_type=F32)
            + jnp.dot(lo, m, preferred_element_type=F32))


def _dot_hi3_left(m, x):
    hi = x.astype(BF16)
    r1 = x - hi.astype(F32)
    mid = r1.astype(BF16)
    lo = (r1 - mid.astype(F32)).astype(BF16)
    return (jnp.dot(m, hi, preferred_element_type=F32)
            + jnp.dot(m, mid, preferred_element_type=F32)
            + jnp.dot(m, lo, preferred_element_type=F32))


def _iota2(shape, dim):
    return lax.broadcasted_iota(jnp.int32, shape, dim)


def _head_mask(rows):
    return _iota2((rows, LANES), 1) < HEAD_DIM


def _stack_heads(x, m0):
    zero = jnp.zeros_like(x)
    return jnp.concatenate([jnp.where(m0, x, zero), jnp.where(m0, zero, x)], axis=0)


def _unstack_heads(xs, m0):
    r = xs.shape[0] // 2
    return jnp.where(m0, xs[:r], xs[r:])


def _group_matrix(scale):
    r = _iota2((LANES, LANES), 0) // HEAD_DIM
    c = _iota2((LANES, LANES), 1) // HEAD_DIM
    return jnp.where(r == c, scale, 0.0).astype(BF16)


def _rms_norm(x, g):
    ms = jnp.mean(x * x, axis=-1, keepdims=True)
    return x * lax.rsqrt(ms + RMS_EPS) * g


def _layer_spec(x, layer, rows=None, row_block=0):
    _, r, c = x.shape
    return pl.BlockSpec((None, rows or r, c), lambda *_: (layer, row_block, 0),
                        pipeline_mode=pl.Buffered(1))


def _proj_retention_kernel(h_ref, g_ref, w_ref, cos_ref, sin_ref, dmask_ref, qdec_ref, kdec_ref, cdec_ref,
                           z_ref, o_ref, s_ref, *, n_pairs, blocks_per_seq, rest_stage_cols):
    @pl.when(pl.program_id(0) % blocks_per_seq == 0)
    def _():
        s_ref[...] = jnp.zeros_like(s_ref)

    n = _rms_norm(h_ref[...], g_ref[...]).astype(BF16)
    tm = n.shape[0]
    c = RET_CHUNK
    n_chunks = tm // c
    half = HEAD_DIM // 2
    m0 = _head_mask(c)
    m0_qk = (_iota2((c, LANES), 1) // half) % PAIR == 0
    gmean = _group_matrix(1.0 / HEAD_DIM)
    key_head = (_iota2((LANES, LANES), 0) // half) % PAIR
    value_head = _iota2((LANES, LANES), 1) // HEAD_DIM
    same_head = key_head == value_head
    pair_cols = 4 * LANES

    def project(c0, c1):
        return jnp.dot(n, w_ref[:, c0:c1], preferred_element_type=F32)

    def project_pair(p):
        parts = []
        for i in range(2):
            both = project(p * pair_cols + 2 * i * LANES, p * pair_cols + 2 * (i + 1) * LANES)
            parts += [both[:, :LANES], both[:, LANES:]]
            yield
        return parts

    def project_rest():
        base = n_pairs * pair_cols
        for c0 in range(0, z_ref.shape[1], rest_stage_cols):
            z_ref[:, c0:c0 + rest_stage_cols] = project(base + c0, base + c0 + rest_stage_cols)
            yield

    def swap_halves(x):
        return pltpu.roll(x, HEAD_DIM, axis=1)

    def retention_pair(p, zq, zk, zv, zg):
        dmask, qdec, kdec, cdec = dmask_ref[p], qdec_ref[p], kdec_ref[p], cdec_ref[p]

        def inner_terms(j):
            rows = slice(j * c, (j + 1) * c)
            cos = cos_ref[rows, :]
            sin = sin_ref[rows, :]
            q = zq[rows] * cos + swap_halves(zq[rows]) * sin
            k = (zk[rows] * cos + swap_halves(zk[rows]) * sin) * (HEAD_DIM ** -0.5)
            v = zv[rows].astype(BF16)
            scores = _dot_nt(_stack_heads(q, m0_qk), k) * dmask
            kv = jnp.where(same_head, _dot((k * kdec).T, v), 0.0)
            yield
            o_inner = _unstack_heads(_dot(scores, v), m0)
            return (q * qdec).astype(BF16), o_inner, kv

        inner = yield from _lockstep_stages(inner_terms(j) for j in range(n_chunks))
        state = s_ref[p]
        states = []
        for _, _, kv in inner:
            states.append(state)
            state = cdec * state + kv
        s_ref[p] = state

        def finish(j):
            rows = slice(j * c, (j + 1) * c)
            q_dec, o_inner, _ = inner[j]
            o = o_inner + _dot(q_dec, states[j])
            yield
            mu = _dot_hi2(o, gmean)
            yield
            d = o - mu
            var = _dot_hi2(d * d, gmean)
            yield
            g = zg[rows]
            out = d * lax.rsqrt(var + RET_GN_EPS) * (g * jax.nn.sigmoid(g))
            o_ref[rows, p * LANES:(p + 1) * LANES] = out.astype(o_ref.dtype)

        yield from _lockstep_stages(finish(j) for j in range(n_chunks))

    projected = _lockstep([project_pair(0)])[0]
    for p in range(n_pairs):
        upcoming = project_pair(p + 1) if p + 1 < n_pairs else project_rest()
        projected = _lockstep([retention_pair(p, *projected), upcoming])[1]


def _retention_tables(dtype):
    c = RET_CHUNK
    n_heads = 8
    h = jnp.arange(n_heads, dtype=dtype)
    log_gamma = jnp.log1p(-(2.0 ** (-5.0 - h)))
    idx = jnp.arange(c, dtype=dtype)
    rel = idx[:, None] - idx[None, :]
    inner = jnp.where(rel >= 0, jnp.exp(jnp.maximum(rel, 0.0)[None] * log_gamma[:, None, None]), 0.0)
    dmask = inner.reshape(n_heads // PAIR, PAIR * c, c)
    pair_lg = log_gamma.reshape(n_heads // PAIR, PAIR)
    lanes_lg = jnp.repeat(pair_lg, HEAD_DIM, axis=1)
    qk_lg = jnp.tile(jnp.repeat(pair_lg, HEAD_DIM // 2, axis=1), (1, 2))
    qdec = jnp.exp((idx + 1.0)[None, :, None] * qk_lg[:, None, :])
    kdec = jnp.exp((c - 1 - idx)[None, :, None] * qk_lg[:, None, :])
    cdec = jnp.broadcast_to(jnp.exp(c * lanes_lg)[:, None, :], (n_heads // PAIR, LANES, LANES))
    return dmask, qdec, kdec, cdec


def _proj_retention(h2, g_all, layer, w_all, sub, cos, sin, seq, n_pairs=4, tm=1024):
    m, d = h2.shape
    n_rest = w_all.shape[2] - n_pairs * 4 * LANES
    blocks_per_seq = seq // tm
    tables = _retention_tables(h2.dtype)
    whole = lambda x: pl.BlockSpec(x.shape, lambda i: (0,) * x.ndim, pipeline_mode=pl.Buffered(1))
    rope = pl.BlockSpec((tm, LANES), lambda i: (i % blocks_per_seq, 0))
    return pl.pallas_call(
        functools.partial(_proj_retention_kernel, n_pairs=n_pairs, blocks_per_seq=blocks_per_seq,
                          rest_stage_cols=2 * LANES),
        grid=(m // tm,),
        in_specs=[pl.BlockSpec((tm, d), lambda i: (i, 0)), _layer_spec(g_all, layer), _layer_spec(w_all, sub),
                  rope, rope] + [whole(t) for t in tables],
        out_specs=[pl.BlockSpec((tm, n_rest), lambda i: (i, 0)),
                   pl.BlockSpec((tm, n_pairs * LANES), lambda i: (i, 0))],
        out_shape=[jax.ShapeDtypeStruct((m, n_rest), F32),
                   jax.ShapeDtypeStruct((m, n_pairs * LANES), BF16)],
        scratch_shapes=[pltpu.VMEM((n_pairs, LANES, LANES), F32)],
        compiler_params=pltpu.CompilerParams(dimension_semantics=("arbitrary",),
                                             vmem_limit_bytes=VMEM_LIMIT),
        name="proj_retention",
    )(h2, g_all, w_all, cos, sin, *tables)


def _lockstep_stages(gens):
    gens = list(gens)
    results = [None] * len(gens)
    live = list(range(len(gens)))
    while live:
        still = []
        for i in live:
            try:
                next(gens[i])
                still.append(i)
            except StopIteration as stop:
                results[i] = stop.value
        live = still
        if live:
            yield
    return results


def _lockstep(gens):
    stages = _lockstep_stages(gens)
    while True:
        try:
            next(stages)
        except StopIteration as stop:
            return stop.value


def _inverse_masks():
    r = _iota2((LANES, LANES), 0)
    c = _iota2((LANES, LANES), 1)
    eye = jnp.where(r == c, 1.0, 0.0).astype(F32)
    diag8 = r // 8 == c // 8
    lower_left = [(r // (2 * k) == c // (2 * k)) & (r // k > c // k) for k in (8, 16, 32)]
    return eye, diag8, lower_left


def _unit_lower_inverse(n_mat, masks):
    eye, diag8, lower_left = masks
    p = jnp.where(diag8, n_mat, 0.0)
    t = eye + p
    p = _dot(p, p)
    yield
    tp = _dot(jnp.concatenate([t, p], axis=0), p)
    t = t + tp[:LANES]
    p = tp[LANES:]
    yield
    t = t + _dot(t, p)
    yield
    for mask in lower_left:
        lt = _dot(jnp.where(mask, n_mat, 0.0), t)
        yield
        t = t + _dot(t, lt)
        yield
    return t


def _delayed(gen, rounds):
    for _ in range(rounds):
        yield
    yield from gen


def _rwkv_kernel(fr_ref, fk_ref, fv_ref, fl_ref, nr_ref, nk_ref, nv_ref, nl_ref,
                 mur_ref, muk_ref, muv_ref, mul_ref,
                 w0_ref, wup_ref, a0_ref, aup_ref, gup_ref, kk_ref, ka_ref, rk_ref, lng_ref, lnb_ref,
                 o_ref,
                 ht_ref, prev_ref, prevl_ref, r_s, lw_s, k_s, v_s, a_s, b_s, bonus_s, gate_s,
                 rhat_s, yhat_s, m_s, g_s, gtot_s, y_s, *, tb):
    t_idx = pl.program_id(2)
    c = RWKV_CHUNK
    n_chunks = tb // c
    gsum = _group_matrix(1.0)
    gmean = _group_matrix(1.0 / HEAD_DIM)
    pr = 2 * c
    first_row = _iota2((pr, 1), 0) == 0

    def prepare(zr_ref, zk_ref, zv_ref, zl_ref, slot):
        def prepare_rows(j):
            rows = slice(j * pr, (j + 1) * pr)

            def mixed(z_ref, mu_ref, carry_ref, carry_row):
                z = z_ref[rows, :]
                before = carry_ref[carry_row:carry_row + 1, :] if j == 0 else z_ref[j * pr - 1:j * pr, :]
                z_prev = jnp.where(first_row, before, pltpu.roll(z, 1, axis=0))
                return z + mu_ref[...] * (z_prev - z)

            xr = mixed(zr_ref, mur_ref, prev_ref, 0)
            xk = mixed(zk_ref, muk_ref, prev_ref, 1)
            xv = mixed(zv_ref, muv_ref, prev_ref, 2)
            xl = mixed(zl_ref, mul_ref, prevl_ref, 0)
            x_wa = xl[:, :LANES]
            w_pre = _dot(jnp.tanh(x_wa), wup_ref[...])
            a_pre = _dot(x_wa, aup_ref[...])
            gate = _dot(jax.nn.sigmoid(xl[:, LANES:]), gup_ref[...])
            kkf = xk * kk_ref[...]
            norm2 = _dot_hi2(kkf * kkf, gsum)
            yield
            w_log = -jax.nn.softplus(-(w0_ref[...] + w_pre)) - 0.5
            a = jax.nn.sigmoid(a0_ref[...] + a_pre)
            kk = kkf / jnp.maximum(jnp.sqrt(norm2), 1e-12)
            k2 = xk * (1.0 + (a - 1.0) * ka_ref[...])
            rk_sum = _dot_hi2(xr * k2 * rk_ref[...], gsum)
            gate_s[slot, rows, :] = gate
            r_s[rows, :] = xr
            lw_s[rows, :] = -jnp.exp(w_log)
            k_s[rows, :] = k2
            v_s[rows, :] = xv
            a_s[rows, :] = -kk
            b_s[rows, :] = kk * a
            yield
            bonus_s[slot, rows, :] = rk_sum * xv

        waiting = [prepare_rows(j) for j in range(tb // pr)]
        running = []
        while waiting or running:
            if waiting:
                running.append(waiting.pop(0))
            running = [g for g in running if next(g, "done") != "done"]
            yield
        prev_ref[0:1, :] = zr_ref[tb - 1:tb, :]
        prev_ref[1:2, :] = zk_ref[tb - 1:tb, :]
        prev_ref[2:3, :] = zv_ref[tb - 1:tb, :]
        prevl_ref[0:1, :] = zl_ref[tb - 1:tb, :]

    @pl.when(t_idx == 0)
    def _():
        for ref in (ht_ref, prev_ref, prevl_ref, bonus_s, gate_s, rhat_s, yhat_s, m_s, g_s, gtot_s):
            ref[...] = jnp.zeros_like(ref)
        for _ in prepare(fr_ref, fk_ref, fv_ref, fl_ref, 0):
            pass

    slot_next = lax.rem(t_idx + 1, 3)
    slot_done = lax.rem(t_idx + 2, 3)

    m0 = _head_mask(c)
    ri = _iota2((LANES, LANES), 0)
    ci = _iota2((LANES, LANES), 1)
    same_head = ri // c == ci // c
    strict = same_head & (ri > ci)
    incl = same_head & (ri >= ci)
    tri = jnp.where(_iota2((c, c), 0) >= _iota2((c, c), 1), 1.0, 0.0).astype(BF16)

    inv_masks = _inverse_masks()

    def chunk_terms(j):
        rows = slice(j * c, (j + 1) * c)
        lw = lw_s[rows, :]
        cum = _dot_hi3_left(tri, lw)
        yield
        cum_last = cum[c - 1:c, :]
        g_inv = jnp.exp(-cum)
        g_rem = jnp.exp(cum_last - cum)
        a_st = _stack_heads(a_s[rows, :] * jnp.exp(cum - lw), m0)
        r_st = _stack_heads(r_s[rows, :] * jnp.exp(cum), m0)
        v_st = _stack_heads(v_s[rows, :], m0)
        bg_st = _stack_heads(b_s[rows, :] * g_rem, m0)
        kg_st = _stack_heads(k_s[rows, :] * g_rem, m0)
        b_t = (b_s[rows, :] * g_inv).astype(BF16)
        k_t = (k_s[rows, :] * g_inv).astype(BF16)
        b2 = jnp.concatenate([b_t, b_t], axis=0)
        k2_ = jnp.concatenate([k_t, k_t], axis=0)
        ar_bf = jnp.concatenate([a_st, r_st], axis=0).astype(BF16)
        v_bf = v_st.astype(BF16)
        ar_b = _dot_nt(ar_bf, b2)
        ar_k = _dot_nt(ar_bf, k2_)
        n_ab = jnp.where(strict, ar_b[:LANES], 0.0)
        n_ak = jnp.where(strict, ar_k[:LANES], 0.0)
        n_rb = jnp.where(incl, ar_b[LANES:], 0.0)
        n_rk = jnp.where(incl, ar_k[LANES:], 0.0)
        yield
        akv = _dot(n_ak, v_bf)
        bk_t = jnp.concatenate([bg_st, kg_st], axis=0).T.astype(BF16)
        t_inv = yield from _unit_lower_inverse(n_ab, inv_masks)
        wu = _dot(t_inv, jnp.concatenate([a_st, akv], axis=1))
        yield
        wu_v = jnp.concatenate(
            [wu.astype(BF16), jnp.concatenate([jnp.zeros_like(v_bf), v_bf], axis=1)], axis=0)
        lhs = jnp.concatenate([jnp.concatenate([n_rb, n_rk], axis=1).astype(BF16), bk_t], axis=0)
        both = jnp.dot(lhs, wu_v, preferred_element_type=F32)
        rw = both[:LANES]
        mg_t = both[LANES:]
        yield
        rhat_s[j] = (r_st + rw[:, :LANES]).astype(BF16)
        yhat_s[j] = rw[:, LANES:]
        m_s[j] = mg_t[:, :LANES].astype(BF16)
        g_s[j] = mg_t[:, LANES:].T
        gtot_s[8 * j:8 * j + 1, :] = jnp.exp(cum_last)

    def chain_and_output():
        ht = ht_ref[...]
        for j in range(n_chunks):
            y_st = _dot_nt(rhat_s[j], ht) + yhat_s[j]
            y_s[j * c:(j + 1) * c, :] = y_st[:c] + y_st[c:]
            ht = ht * gtot_s[8 * j:8 * j + 1, :] + _dot_nt(ht, m_s[j]) + g_s[j]
            yield
        ht_ref[...] = ht
        y = y_s[...]
        mu = _dot_hi2(y, gmean)
        yield
        d = y - mu
        var = _dot_hi2(d * d, gmean)
        yield
        yn = d * lax.rsqrt(var + RWKV_GN_EPS) * lng_ref[...] + lnb_ref[...]
        o_ref[...] = ((yn + bonus_s[slot_done]) * gate_s[slot_done]).astype(o_ref.dtype)

    _lockstep([chunk_terms(j) for j in range(n_chunks)]
              + [chain_and_output(), _delayed(prepare(nr_ref, nk_ref, nv_ref, nl_ref, slot_next), 2)])


def _rwkv(z, params, layer, batch, seq, col, tb=512):
    nt = seq // tb
    n_pairs = 4

    def zspec(name, block_of_step, width=LANES):
        off = col[name]
        if width == LANES:
            return pl.BlockSpec((tb, LANES), lambda b, p, t: (b * nt + block_of_step(t), off + p))
        return pl.BlockSpec((tb, width), lambda b, p, t: (b * nt + block_of_step(t), off * LANES // width))

    first_block = lambda t: 0
    next_block = lambda t: jnp.minimum(t + 1, nt - 1)

    def pair_row(x, first_block=0):
        return x, pl.BlockSpec((None, 1, LANES), lambda b, p, t: (layer, 0, first_block + p))

    def pair_cols(x):
        return x, pl.BlockSpec((None, x.shape[1], LANES), lambda b, p, t: (layer, 0, p))

    mu = params["mu"]
    rw_blocks = 512 // LANES
    mu_lora = (mu, pl.BlockSpec((None, 1, 2 * LANES), lambda b, p, t: (layer, 0, 3 * rw_blocks // 2)))

    args, specs = [z] * 8, []
    for block_of_step in (first_block, next_block):
        specs += [zspec("rr", block_of_step), zspec("kr", block_of_step), zspec("vr", block_of_step),
                  zspec("lora", block_of_step, 2 * LANES)]
    for a, s in (pair_row(mu, 0), pair_row(mu, rw_blocks), pair_row(mu, 2 * rw_blocks), mu_lora,
                 pair_row(params["w0"]), pair_cols(params["w_up"]),
                 pair_row(params["a0"]), pair_cols(params["a_up"]), pair_cols(params["g_up"]),
                 pair_row(params["k_k"]), pair_row(params["k_a"]), pair_row(params["r_k"]),
                 pair_row(params["ln_g"]), pair_row(params["ln_b"])):
        args.append(a)
        specs.append(s)

    n_chunks = tb // RWKV_CHUNK
    blk = pltpu.VMEM((tb, LANES), F32)
    ring = pltpu.VMEM((3, tb, LANES), F32)
    mat = lambda dtype: pltpu.VMEM((n_chunks, LANES, LANES), dtype)
    return pl.pallas_call(
        functools.partial(_rwkv_kernel, tb=tb),
        grid=(batch, n_pairs, nt + 1),
        in_specs=specs,
        out_specs=pl.BlockSpec((tb, LANES), lambda b, p, t: (b * nt + jnp.maximum(t - 1, 0), p)),
        out_shape=jax.ShapeDtypeStruct((batch * seq, n_pairs * LANES), BF16),
        scratch_shapes=[pltpu.VMEM((LANES, LANES), F32),
                        pltpu.VMEM((8, LANES), F32),
                        pltpu.VMEM((8, 2 * LANES), F32),
                        blk, blk, blk, blk, blk, blk,
                        ring, ring,
                        mat(BF16), mat(F32), mat(BF16), mat(F32),
                        pltpu.VMEM((8 * n_chunks, LANES), F32),
                        blk],
        compiler_params=pltpu.CompilerParams(
            dimension_semantics=("parallel", "parallel", "arbitrary"),
            vmem_limit_bytes=VMEM_LIMIT),
        name="rwkv7",
    )(*args)


def _proj_swa_kernel(sink_ref, h_ref, g_ref, w_ref, b_ref, o_ref, carry_ref, *,
                     n_kv, blocks_per_seq, layer_sink_base):
    starts_sequence = pl.program_id(0) % blocks_per_seq == 0

    @pl.when(starts_sequence)
    def _():
        carry_ref[...] = jnp.zeros_like(carry_ref)

    n = _rms_norm(h_ref[...], g_ref[...]).astype(BF16)
    tm = n.shape[0]
    w = SWA_WINDOW
    n_windows = tm // w
    m0 = _head_mask(w)
    keep_head0 = jnp.where(m0, 1.0, 0.0).astype(BF16)
    keep_head1 = jnp.where(m0, 0.0, 1.0).astype(BF16)
    from_prev = _iota2((PAIR * w, w), 1) > _iota2((PAIR * w, w), 0) % w
    first_bias = jnp.where(starts_sequence, -jnp.inf, 0.0).astype(F32)
    second_head = _iota2((PAIR * w, 1), 0) >= w
    head_cols = 4 * LANES

    def project_head(hd):
        parts = []
        for i in range(2):
            cols = slice(hd * head_cols + 2 * i * LANES, hd * head_cols + 2 * (i + 1) * LANES)
            z = jnp.dot(n, w_ref[:, cols], preferred_element_type=F32) + b_ref[:, cols]
            parts.append(z.astype(BF16))
            yield
        return parts

    def attend(hd, win, j, q4, kv):
        qb = 2 * hd + j
        rows = slice(win * w, (win + 1) * w)
        if win == 0:
            k_prev, v_prev = carry_ref[:, 2 * hd * LANES:(2 * hd + 1) * LANES], carry_ref[:, (2 * hd + 1) * LANES:(2 * hd + 2) * LANES]
            prev_bias = first_bias
        else:
            before = slice((win - 1) * w, win * w)
            k_prev, v_prev = kv[before, :LANES], kv[before, LANES:]
            prev_bias = 0.0
        q = q4[rows, j * LANES:(j + 1) * LANES] * (HEAD_DIM ** -0.5)
        kd = jnp.concatenate([k_prev, kv[rows, :LANES]], axis=0)
        vd = jnp.concatenate([v_prev, kv[rows, LANES:]], axis=0)
        q_stacked = jnp.concatenate([q * keep_head0, q * keep_head1], axis=0)
        s = _dot_nt(q_stacked, kd)
        yield
        sm = jnp.where(from_prev, s[:, :w] + prev_bias, s[:, w:])
        sink = jnp.where(second_head, sink_ref[layer_sink_base + 2 * qb + 1],
                         sink_ref[layer_sink_base + 2 * qb])
        m = jnp.maximum(jnp.max(sm, axis=-1, keepdims=True), sink)
        e = jnp.exp(sm - m)
        denom = jnp.sum(e, axis=-1, keepdims=True) + jnp.exp(sink - m)
        e_split = jnp.concatenate([jnp.where(from_prev, e, 0.0).astype(BF16),
                                   jnp.where(from_prev, 0.0, e).astype(BF16)], axis=1)
        pv = _dot(e_split, vd)
        yield
        o_ref[rows, qb * LANES:(qb + 1) * LANES] = _unstack_heads(pv / denom, m0).astype(o_ref.dtype)

    projected = _lockstep([project_head(0)])[0]
    for hd in range(n_kv):
        work = [attend(hd, win, j, *projected) for win in range(n_windows) for j in range(2)]
        if hd + 1 < n_kv:
            work.append(project_head(hd + 1))
        done = _lockstep(work)
        carry_ref[:, 2 * hd * LANES:(2 * hd + 2) * LANES] = projected[1][tm - w:tm, :]
        projected = done[-1]


def _proj_swa(h2, g_all, layer, w_all, b_all, sub, sinks_flat, seq, n_heads=16, n_kv=4, tm=1024):
    m, d = h2.shape
    qw = n_heads * HEAD_DIM
    blocks_per_seq = seq // tm
    return pl.pallas_call(
        functools.partial(_proj_swa_kernel, n_kv=n_kv, blocks_per_seq=blocks_per_seq,
                          layer_sink_base=sub * n_heads),
        grid=(m // tm,),
        in_specs=[pl.BlockSpec(memory_space=pltpu.SMEM),
                  pl.BlockSpec((tm, d), lambda i: (i, 0)), _layer_spec(g_all, layer),
                  _layer_spec(w_all, sub), _layer_spec(b_all, sub)],
        out_specs=pl.BlockSpec((tm, qw), lambda i: (i, 0)),
        out_shape=jax.ShapeDtypeStruct((m, qw), BF16),
        scratch_shapes=[pltpu.VMEM((SWA_WINDOW, n_kv * PAIR * LANES), BF16)],
        compiler_params=pltpu.CompilerParams(dimension_semantics=("arbitrary",),
                                             vmem_limit_bytes=VMEM_LIMIT),
        name="proj_swa",
    )(sinks_flat, h2, g_all, w_all, b_all)


def _post_kernel(*refs, n_mix, has_bias, has_final):
    it = iter(refs)
    h_ref = next(it)
    mix_refs = [next(it) for _ in range(n_mix)]
    wo_refs = [next(it) for _ in range(n_mix)]
    bo_ref = next(it) if has_bias else None
    g2_ref, wg_ref, wu_ref, wd_ref = next(it), next(it), next(it), next(it)
    gf_ref = next(it) if has_final else None
    o_ref = next(it)

    h = h_ref[...]
    for m_ref, w_ref in zip(mix_refs, wo_refs):
        h = h + jnp.dot(m_ref[...].astype(BF16), w_ref[...], preferred_element_type=F32)
    if has_bias:
        h = h + bo_ref[...]
    n = _rms_norm(h, g2_ref[...]).astype(BF16)
    gate = jnp.dot(n, wg_ref[...], preferred_element_type=F32)
    up = jnp.dot(n, wu_ref[...], preferred_element_type=F32)
    act = (gate * jax.nn.sigmoid(gate) * up).astype(BF16)
    h = h + jnp.dot(act, wd_ref[...], preferred_element_type=F32)
    if has_final:
        h = _rms_norm(h, gf_ref[...])
    o_ref[...] = h


def _post(h2, mixes, wo_all, sub, bo_all, g2_all, wg_all, wu_all, wd_all, layer, final_g, tm=512):
    m, d = h2.shape
    row = lambda width: pl.BlockSpec((tm, width), lambda i: (i, 0))
    args, specs = [h2], [row(d)]
    for mx in mixes:
        args.append(mx)
        specs.append(row(mx.shape[1]))
    for k, mx in enumerate(mixes):
        args.append(wo_all)
        specs.append(_layer_spec(wo_all, sub, rows=mx.shape[1], row_block=k))
    if bo_all is not None:
        args.append(bo_all)
        specs.append(_layer_spec(bo_all, sub))
    for a in (g2_all, wg_all, wu_all, wd_all):
        args.append(a)
        specs.append(_layer_spec(a, layer))
    if final_g is not None:
        args.append(final_g.reshape(1, d))
        specs.append(pl.BlockSpec((1, d), lambda i: (0, 0), pipeline_mode=pl.Buffered(1)))
    bo = bo_all
    return pl.pallas_call(
        functools.partial(_post_kernel, n_mix=len(mixes), has_bias=bo is not None,
                          has_final=final_g is not None),
        grid=(m // tm,),
        in_specs=specs,
        out_specs=row(d),
        out_shape=jax.ShapeDtypeStruct((m, d), F32),
        compiler_params=pltpu.CompilerParams(dimension_semantics=("parallel",),
                                             vmem_limit_bytes=VMEM_LIMIT),
        name="post_ffn",
    )(*args)


def _even_layout(w_in, w_up, a_up):
    n_layers, d, _ = w_in.shape
    rw = 512

    def interleave_halves(w):
        w6 = w.reshape(n_layers, d, rw // LANES, PAIR, 2, HEAD_DIM // 2)
        return w6.transpose(0, 1, 2, 4, 3, 5).reshape(n_layers, d, rw)

    q, k = interleave_halves(w_in[:, :, :rw]), interleave_halves(w_in[:, :, rw:2 * rw])
    v, g = w_in[:, :, 2 * rw:3 * rw], w_in[:, :, 3 * rw:4 * rw]
    per_pair = [x[:, :, p * LANES:(p + 1) * LANES] for p in range(rw // LANES) for x in (q, k, v, g)]
    w = jnp.concatenate(per_pair + [w_in[:, :, 4 * rw:]], axis=2).astype(BF16)
    names = ["rr", "kr", "vr", "lora"]
    col = {nm: i * (rw // LANES) for i, nm in enumerate(names)}
    zeros = jnp.zeros_like(w_up)
    w_up_pad = jnp.concatenate([w_up, zeros], axis=1).astype(BF16)
    a_up_pad = jnp.concatenate([zeros, a_up], axis=1).astype(BF16)
    return w, col, w_up_pad, a_up_pad


def _swa_layout(w_qkv, b_qkv, n_heads=16, n_kv=4):
    qw = n_heads * HEAD_DIM
    kw = n_kv * HEAD_DIM
    group = qw // n_kv

    def layout(x):
        q, k, v = x[..., :qw], x[..., qw:qw + kw], x[..., qw + kw:]
        parts = []
        for hd in range(n_kv):
            k_h = k[..., hd * HEAD_DIM:(hd + 1) * HEAD_DIM]
            v_h = v[..., hd * HEAD_DIM:(hd + 1) * HEAD_DIM]
            parts += [q[..., hd * group:(hd + 1) * group], k_h, k_h, v_h, v_h]
        return jnp.concatenate(parts, axis=-1)

    return layout(w_qkv).astype(BF16), layout(b_qkv)[:, None, :]


def _rope_tables(seq, dtype):
    half = HEAD_DIM // 2
    inv_freq = ROPE_BASE ** (-jnp.linspace(0.0, 1.0, half, dtype=dtype))
    ang = jnp.arange(seq, dtype=dtype)[:, None] * inv_freq[None, :]
    sin = jnp.sin(ang)
    cos = jnp.cos(ang)
    return (jnp.tile(cos, (1, LANES // half)),
            jnp.concatenate([-sin] * PAIR + [sin] * PAIR, axis=1))


def kernel(x, norm1_g, norm2_g, final_g, even_w_in, even_w_out, rwkv_mu, rwkv_w0, rwkv_w_up,
           rwkv_a0, rwkv_a_up, rwkv_g_up, rwkv_k_k, rwkv_k_a, rwkv_r_k, rwkv_ln_g, rwkv_ln_b,
           swa_w_qkv, swa_b_qkv, swa_sinks, swa_w_o, swa_b_o, ffn_w_gate, ffn_w_up, ffn_w_down):
    batch, seq, d = x.shape
    depth = norm1_g.shape[0]
    h = x.reshape(batch * seq, d)
    cos, sin = _rope_tables(seq, x.dtype)
    sinks_flat = swa_sinks.reshape(-1)

    def rows(p):
        return p.reshape(p.shape[0], 1, -1)

    even_w, col, w_up_pad, a_up_pad = _even_layout(even_w_in, rwkv_w_up, rwkv_a_up)
    rwkv_params = dict(mu=rows(rwkv_mu), w0=rows(rwkv_w0), w_up=w_up_pad, a0=rows(rwkv_a0), a_up=a_up_pad,
                       g_up=rwkv_g_up.astype(BF16), k_k=rows(rwkv_k_k), k_a=rows(rwkv_k_a),
                       r_k=rows(rwkv_r_k), ln_g=rows(rwkv_ln_g), ln_b=rows(rwkv_ln_b))
    swa_w, swa_b = _swa_layout(swa_w_qkv, swa_b_qkv)
    g1, g2 = rows(norm1_g), rows(norm2_g)
    even_wo, swa_wo, swa_bo = even_w_out.astype(BF16), swa_w_o.astype(BF16), rows(swa_b_o)
    wg, wu, wd = ffn_w_gate.astype(BF16), ffn_w_up.astype(BF16), ffn_w_down.astype(BF16)

    for layer in range(depth):
        i = layer // 2
        if layer % 2 == 0:
            z, ret_out = _proj_retention(h, g1, layer, even_w, i, cos, sin, seq)
            mixes = [ret_out, _rwkv(z, rwkv_params, i, batch, seq, col)]
            wo, bo = even_wo, None
        else:
            mixes = [_proj_swa(h, g1, layer, swa_w, swa_b, i, sinks_flat, seq)]
            wo, bo = swa_wo, swa_bo
        h = _post(h, mixes, wo, i, bo, g2, wg, wu, wd, layer, final_g if layer == depth - 1 else None)
    return h.reshape(batch, seq, d)
```

```python
import functools

import jax
import jax.numpy as jnp
from jax import lax
from jax.experimental import pallas as pl
from jax.experimental.pallas import tpu as pltpu

F32 = jnp.float32
BF16 = jnp.bfloat16

LANES = 128
SUBLANES = 8
HEAD_DIM = 64
PAIR = LANES // HEAD_DIM
GROUP_WIDTH = 8 * HEAD_DIM
RMS_EPS = 1e-6
RET_GN_EPS = 1e-6
RWKV_GN_EPS = 64e-5
ROPE_BASE = 10000.0
RET_CHUNK = 128
RWKV_CHUNK = 64
SWA_WINDOW = 128
VMEM_LIMIT = 56 * 1024 * 1024


def _dot(a, b):
    return jnp.dot(a.astype(BF16), b.astype(BF16), preferred_element_type=F32)


def _dot_nt(a, b):
    return lax.dot_general(a.astype(BF16), b.astype(BF16), (((1,), (1,)), ((), ())),
                           preferred_element_type=F32)


def _split2(x):
    hi = x.astype(BF16)
    lo = (x - hi.astype(F32)).astype(BF16)
    return hi, lo


def _dot_hi2(x, m):
    hi, lo = _split2(x)
    return (jnp.dot(hi, m, preferred_element_type=F32)
            + jnp.dot(lo, m, preferred_element_type=F32))


def _dot_hi3_left(m, x):
    hi = x.astype(BF16)
    r1 = x - hi.astype(F32)
    mid = r1.astype(BF16)
    lo = (r1 - mid.astype(F32)).astype(BF16)
    return (jnp.dot(m, hi, preferred_element_type=F32)
            + jnp.dot(m, mid, preferred_element_type=F32)
            + jnp.dot(m, lo, preferred_element_type=F32))


def _iota2(shape, dim):
    return lax.broadcasted_iota(jnp.int32, shape, dim)


def _head_mask(rows):
    return _iota2((rows, LANES), 1) < HEAD_DIM


def _stack_heads(x, m0):
    zero = jnp.zeros_like(x)
    return jnp.concatenate([jnp.where(m0, x, zero), jnp.where(m0, zero, x)], axis=0)


def _unstack_heads(xs, m0):
    r = xs.shape[0] // 2
    return jnp.where(m0, xs[:r], xs[r:])


def _group_matrix(scale):
    r = _iota2((LANES, LANES), 0) // HEAD_DIM
    c = _iota2((LANES, LANES), 1) // HEAD_DIM
    return jnp.where(r == c, scale, 0.0).astype(BF16)


def _rms_norm(x, g):
    ms = jnp.mean(x * x, axis=-1, keepdims=True)
    return x * lax.rsqrt(ms + RMS_EPS) * g


def _layer_spec(x, layer, rows=None, row_block=0):
    _, r, c = x.shape
    return pl.BlockSpec((None, rows or r, c), lambda *_: (layer, row_block, 0),
                        pipeline_mode=pl.Buffered(1))


def _proj_retention_kernel(h_ref, g_ref, w_ref, cos_ref, sin_ref, dmask_ref, qdec_ref, kdec_ref, cdec_ref,
                           z_ref, o_ref, s_ref, *, n_pairs, blocks_per_seq, rest_stage_cols):
    @pl.when(pl.program_id(0) % blocks_per_seq == 0)
    def _():
        s_ref[...] = jnp.zeros_like(s_ref)

    n = _rms_norm(h_ref[...], g_ref[...]).astype(BF16)
    tm = n.shape[0]
    c = RET_CHUNK
    n_chunks = tm // c
    half = HEAD_DIM // 2
    m0 = _head_mask(c)
    m0_qk = (_iota2((c, LANES), 1) // half) % PAIR == 0
    gmean = _group_matrix(1.0 / HEAD_DIM)
    key_head = (_iota2((LANES, LANES), 0) // half) % PAIR
    value_head = _iota2((LANES, LANES), 1) // HEAD_DIM
    same_head = key_head == value_head
    pair_cols = 4 * LANES

    def project(c0, c1):
        return jnp.dot(n, w_ref[:, c0:c1], preferred_element_type=F32)

    def project_pair(p):
        parts = []
        for i in range(2):
            both = project(p * pair_cols + 2 * i * LANES, p * pair_cols + 2 * (i + 1) * LANES)
            parts += [both[:, :LANES], both[:, LANES:]]
            yield
        return parts

    def project_rest():
        base = n_pairs * pair_cols
        for c0 in range(0, z_ref.shape[1], rest_stage_cols):
            z_ref[:, c0:c0 + rest_stage_cols] = project(base + c0, base + c0 + rest_stage_cols)
            yield

    def swap_halves(x):
        return pltpu.roll(x, HEAD_DIM, axis=1)

    def retention_pair(p, zq, zk, zv, zg):
        dmask, qdec, kdec, cdec = dmask_ref[p], qdec_ref[p], kdec_ref[p], cdec_ref[p]

        def inner_terms(j):
            rows = slice(j * c, (j + 1) * c)
            cos = cos_ref[rows, :]
            sin = sin_ref[rows, :]
            q = zq[rows] * cos + swap_halves(zq[rows]) * sin
            k = (zk[rows] * cos + swap_halves(zk[rows]) * sin) * (HEAD_DIM ** -0.5)
            v = zv[rows].astype(BF16)
            scores = _dot_nt(_stack_heads(q, m0_qk), k) * dmask
            kv = jnp.where(same_head, _dot((k * kdec).T, v), 0.0)
            yield
            o_inner = _unstack_heads(_dot(scores, v), m0)
            return (q * qdec).astype(BF16), o_inner, kv

        inner = yield from _lockstep_stages(inner_terms(j) for j in range(n_chunks))
        state = s_ref[p]
        states = []
        for _, _, kv in inner:
            states.append(state)
            state = cdec * state + kv
        s_ref[p] = state

        def finish(j):
            rows = slice(j * c, (j + 1) * c)
            q_dec, o_inner, _ = inner[j]
            o = o_inner + _dot(q_dec, states[j])
            yield
            mu = _dot_hi2(o, gmean)
            yield
            d = o - mu
            var = _dot_hi2(d * d, gmean)
            yield
            g = zg[rows]
            out = d * lax.rsqrt(var + RET_GN_EPS) * (g * jax.nn.sigmoid(g))
            o_ref[rows, p * LANES:(p + 1) * LANES] = out.astype(o_ref.dtype)

        yield from _lockstep_stages(finish(j) for j in range(n_chunks))

    rest = project_rest()
    n_rest_stages = z_ref.shape[1] // rest_stage_cols

    def upcoming(p):
        parts = None
        if p + 1 < n_pairs:
            parts = yield from project_pair(p + 1)
        first = n_rest_stages * p // n_pairs
        last = n_rest_stages * (p + 1) // n_pairs
        for _ in range(first, last):
            next(rest)
            yield
        return parts

    projected = _lockstep([project_pair(0)])[0]
    for p in range(n_pairs):
        projected = _lockstep([retention_pair(p, *projected), upcoming(p)])[1]


def _retention_tables(dtype):
    c = RET_CHUNK
    n_heads = GROUP_WIDTH // HEAD_DIM
    h = jnp.arange(n_heads, dtype=dtype)
    log_gamma = jnp.log1p(-(2.0 ** (-5.0 - h)))
    idx = jnp.arange(c, dtype=dtype)
    rel = idx[:, None] - idx[None, :]
    inner = jnp.where(rel >= 0, jnp.exp(jnp.maximum(rel, 0.0)[None] * log_gamma[:, None, None]), 0.0)
    dmask = inner.reshape(n_heads // PAIR, PAIR * c, c)
    pair_lg = log_gamma.reshape(n_heads // PAIR, PAIR)
    lanes_lg = jnp.repeat(pair_lg, HEAD_DIM, axis=1)
    qk_lg = jnp.tile(jnp.repeat(pair_lg, HEAD_DIM // 2, axis=1), (1, 2))
    qdec = jnp.exp((idx + 1.0)[None, :, None] * qk_lg[:, None, :])
    kdec = jnp.exp((c - 1 - idx)[None, :, None] * qk_lg[:, None, :])
    cdec = jnp.broadcast_to(jnp.exp(c * lanes_lg)[:, None, :], (n_heads // PAIR, LANES, LANES))
    return dmask, qdec, kdec, cdec


def _proj_retention(h2, g_all, layer, w_all, sub, cos, sin, seq, n_pairs=4, tm=1024):
    m, d = h2.shape
    n_rest = w_all.shape[2] - n_pairs * 4 * LANES
    blocks_per_seq = seq // tm
    tables = _retention_tables(h2.dtype)
    whole = lambda x: pl.BlockSpec(x.shape, lambda i: (0,) * x.ndim, pipeline_mode=pl.Buffered(1))
    rope = pl.BlockSpec((tm, LANES), lambda i: (i % blocks_per_seq, 0))
    return pl.pallas_call(
        functools.partial(_proj_retention_kernel, n_pairs=n_pairs, blocks_per_seq=blocks_per_seq,
                          rest_stage_cols=2 * LANES),
        grid=(m // tm,),
        in_specs=[pl.BlockSpec((tm, d), lambda i: (i, 0)), _layer_spec(g_all, layer), _layer_spec(w_all, sub),
                  rope, rope] + [whole(t) for t in tables],
        out_specs=[pl.BlockSpec((tm, n_rest), lambda i: (i, 0)),
                   pl.BlockSpec((tm, n_pairs * LANES), lambda i: (i, 0))],
        out_shape=[jax.ShapeDtypeStruct((m, n_rest), F32),
                   jax.ShapeDtypeStruct((m, n_pairs * LANES), BF16)],
        scratch_shapes=[pltpu.VMEM((n_pairs, LANES, LANES), F32)],
        compiler_params=pltpu.CompilerParams(dimension_semantics=("arbitrary",),
                                             vmem_limit_bytes=VMEM_LIMIT),
        name="proj_retention",
    )(h2, g_all, w_all, cos, sin, *tables)


def _lockstep_stages(gens):
    gens = list(gens)
    results = [None] * len(gens)
    live = list(range(len(gens)))
    while live:
        still = []
        for i in live:
            try:
                next(gens[i])
                still.append(i)
            except StopIteration as stop:
                results[i] = stop.value
        live = still
        if live:
            yield
    return results


def _lockstep(gens):
    stages = _lockstep_stages(gens)
    while True:
        try:
            next(stages)
        except StopIteration as stop:
            return stop.value


def _inverse_masks():
    r = _iota2((LANES, LANES), 0)
    c = _iota2((LANES, LANES), 1)
    eye = jnp.where(r == c, 1.0, 0.0).astype(F32)
    diag8 = r // 8 == c // 8
    lower_left = [(r // (2 * k) == c // (2 * k)) & (r // k > c // k) for k in (8, 16, 32)]
    return eye, diag8, lower_left


def _unit_lower_inverse(n_mat, masks):
    eye, diag8, lower_left = masks
    p = jnp.where(diag8, n_mat, 0.0)
    t = eye + p
    p = _dot(p, p)
    yield
    tp = _dot(jnp.concatenate([t, p], axis=0), p)
    t = t + tp[:LANES]
    p = tp[LANES:]
    yield
    t = t + _dot(t, p)
    yield
    for mask in lower_left:
        lt = _dot(jnp.where(mask, n_mat, 0.0), t)
        yield
        t = t + _dot(t, lt)
        yield
    return t


def _delayed(gen, rounds):
    for _ in range(rounds):
        yield
    yield from gen


def _rwkv_kernel(fr_ref, fk_ref, fv_ref, fl_ref, nr_ref, nk_ref, nv_ref, nl_ref,
                 mur_ref, muk_ref, muv_ref, mul_ref,
                 w0_ref, wup_ref, a0_ref, aup_ref, gup_ref, kk_ref, ka_ref, rk_ref, lng_ref, lnb_ref,
                 o_ref,
                 ht_ref, prev_ref, prevl_ref, r_s, lw_s, k_s, v_s, a_s, b_s, bonus_s, gate_s,
                 rhat_s, yhat_s, m_s, g_s, gtot_s, y_s, *, tb):
    t_idx = pl.program_id(2)
    c = RWKV_CHUNK
    n_chunks = tb // c
    gsum = _group_matrix(1.0)
    gmean = _group_matrix(1.0 / HEAD_DIM)
    pr = 2 * c
    first_row = _iota2((pr, 1), 0) == 0

    def prepare(zr_ref, zk_ref, zv_ref, zl_ref, slot):
        def prepare_rows(j):
            rows = slice(j * pr, (j + 1) * pr)

            def mixed(z_ref, mu_ref, carry_ref, carry_row):
                z = z_ref[rows, :]
                before = carry_ref[carry_row:carry_row + 1, :] if j == 0 else z_ref[j * pr - 1:j * pr, :]
                z_prev = jnp.where(first_row, before, pltpu.roll(z, 1, axis=0))
                return z + mu_ref[...] * (z_prev - z)

            xr = mixed(zr_ref, mur_ref, prev_ref, 0)
            xk = mixed(zk_ref, muk_ref, prev_ref, 1)
            xv = mixed(zv_ref, muv_ref, prev_ref, 2)
            xl = mixed(zl_ref, mul_ref, prevl_ref, 0)
            x_wa = xl[:, :LANES]
            w_pre = _dot(jnp.tanh(x_wa), wup_ref[...])
            a_pre = _dot(x_wa, aup_ref[...])
            gate = _dot(jax.nn.sigmoid(xl[:, LANES:]), gup_ref[...])
            kkf = xk * kk_ref[...]
            norm2 = _dot_hi2(kkf * kkf, gsum)
            yield
            w_log = -jax.nn.softplus(-(w0_ref[...] + w_pre)) - 0.5
            a = jax.nn.sigmoid(a0_ref[...] + a_pre)
            kk = kkf / jnp.maximum(jnp.sqrt(norm2), 1e-12)
            k2 = xk * (1.0 + (a - 1.0) * ka_ref[...])
            rk_sum = _dot_hi2(xr * k2 * rk_ref[...], gsum)
            gate_s[slot, rows, :] = gate
            r_s[rows, :] = xr
            lw_s[rows, :] = -jnp.exp(w_log)
            k_s[rows, :] = k2
            v_s[rows, :] = xv
            a_s[rows, :] = -kk
            b_s[rows, :] = kk * a
            yield
            bonus_s[slot, rows, :] = rk_sum * xv

        waiting = [prepare_rows(j) for j in range(tb // pr)]
        running = []
        while waiting or running:
            if waiting:
                running.append(waiting.pop(0))
            running = [g for g in running if next(g, "done") != "done"]
            yield
        prev_ref[0:1, :] = zr_ref[tb - 1:tb, :]
        prev_ref[1:2, :] = zk_ref[tb - 1:tb, :]
        prev_ref[2:3, :] = zv_ref[tb - 1:tb, :]
        prevl_ref[0:1, :] = zl_ref[tb - 1:tb, :]

    @pl.when(t_idx == 0)
    def _():
        for ref in (ht_ref, prev_ref, prevl_ref, bonus_s, gate_s, rhat_s, yhat_s, m_s, g_s, gtot_s):
            ref[...] = jnp.zeros_like(ref)
        for _ in prepare(fr_ref, fk_ref, fv_ref, fl_ref, 0):
            pass

    slot_next = lax.rem(t_idx + 1, 3)
    slot_done = lax.rem(t_idx + 2, 3)

    m0 = _head_mask(c)
    ri = _iota2((LANES, LANES), 0)
    ci = _iota2((LANES, LANES), 1)
    same_head = ri // c == ci // c
    strict = same_head & (ri > ci)
    incl = same_head & (ri >= ci)
    tri = jnp.where(_iota2((c, c), 0) >= _iota2((c, c), 1), 1.0, 0.0).astype(BF16)

    inv_masks = _inverse_masks()

    def chunk_terms(j):
        rows = slice(j * c, (j + 1) * c)
        lw = lw_s[rows, :]
        cum = _dot_hi3_left(tri, lw)
        yield
        cum_last = cum[c - 1:c, :]
        g_inv = jnp.exp(-cum)
        g_rem = jnp.exp(cum_last - cum)
        a_st = _stack_heads(a_s[rows, :] * jnp.exp(cum - lw), m0)
        r_st = _stack_heads(r_s[rows, :] * jnp.exp(cum), m0)
        v_st = _stack_heads(v_s[rows, :], m0)
        bg_st = _stack_heads(b_s[rows, :] * g_rem, m0)
        kg_st = _stack_heads(k_s[rows, :] * g_rem, m0)
        b_t = (b_s[rows, :] * g_inv).astype(BF16)
        k_t = (k_s[rows, :] * g_inv).astype(BF16)
        b2 = jnp.concatenate([b_t, b_t], axis=0)
        k2_ = jnp.concatenate([k_t, k_t], axis=0)
        ar_bf = jnp.concatenate([a_st, r_st], axis=0).astype(BF16)
        v_bf = v_st.astype(BF16)
        ar_b = _dot_nt(ar_bf, b2)
        ar_k = _dot_nt(ar_bf, k2_)
        n_ab = jnp.where(strict, ar_b[:LANES], 0.0)
        n_ak = jnp.where(strict, ar_k[:LANES], 0.0)
        n_rb = jnp.where(incl, ar_b[LANES:], 0.0)
        n_rk = jnp.where(incl, ar_k[LANES:], 0.0)
        yield
        akv = _dot(n_ak, v_bf)
        bk_t = jnp.concatenate([bg_st, kg_st], axis=0).T.astype(BF16)
        t_inv = yield from _unit_lower_inverse(n_ab, inv_masks)
        wu = _dot(t_inv, jnp.concatenate([a_st, akv], axis=1))
        yield
        wu_v = jnp.concatenate(
            [wu.astype(BF16), jnp.concatenate([jnp.zeros_like(v_bf), v_bf], axis=1)], axis=0)
        lhs = jnp.concatenate([jnp.concatenate([n_rb, n_rk], axis=1).astype(BF16), bk_t], axis=0)
        both = jnp.dot(lhs, wu_v, preferred_element_type=F32)
        rw = both[:LANES]
        mg_t = both[LANES:]
        yield
        rhat_s[j] = (r_st + rw[:, :LANES]).astype(BF16)
        yhat_s[j] = rw[:, LANES:]
        m_s[j] = mg_t[:, :LANES].astype(BF16)
        g_s[j] = mg_t[:, LANES:].T
        gtot_s[SUBLANES * j:SUBLANES * j + 1, :] = jnp.exp(cum_last)

    def chain_and_output():
        ht = ht_ref[...]
        for j in range(n_chunks):
            y_st = _dot_nt(rhat_s[j], ht) + yhat_s[j]
            y_s[j * c:(j + 1) * c, :] = y_st[:c] + y_st[c:]
            ht = ht * gtot_s[SUBLANES * j:SUBLANES * j + 1, :] + _dot_nt(ht, m_s[j]) + g_s[j]
            yield
        ht_ref[...] = ht
        y = y_s[...]
        mu = _dot_hi2(y, gmean)
        yield
        d = y - mu
        var = _dot_hi2(d * d, gmean)
        yield
        yn = d * lax.rsqrt(var + RWKV_GN_EPS) * lng_ref[...] + lnb_ref[...]
        o_ref[...] = ((yn + bonus_s[slot_done]) * gate_s[slot_done]).astype(o_ref.dtype)

    _lockstep([chunk_terms(j) for j in range(n_chunks)]
              + [chain_and_output(), _delayed(prepare(nr_ref, nk_ref, nv_ref, nl_ref, slot_next), 2)])


def _rwkv(z, params, layer, batch, seq, col, tb=512):
    nt = seq // tb
    n_pairs = 4

    def zspec(name, block_of_step, width=LANES):
        off = col[name]
        if width == LANES:
            return pl.BlockSpec((tb, LANES), lambda b, p, t: (b * nt + block_of_step(t), off + p))
        return pl.BlockSpec((tb, width), lambda b, p, t: (b * nt + block_of_step(t), off * LANES // width))

    first_block = lambda t: 0
    next_block = lambda t: jnp.minimum(t + 1, nt - 1)

    def pair_row(x, first_block=0):
        return x, pl.BlockSpec((None, 1, LANES), lambda b, p, t: (layer, 0, first_block + p))

    def pair_cols(x):
        return x, pl.BlockSpec((None, x.shape[1], LANES), lambda b, p, t: (layer, 0, p))

    mu = params["mu"]
    rw_blocks = GROUP_WIDTH // LANES
    mu_lora = (mu, pl.BlockSpec((None, 1, 2 * LANES), lambda b, p, t: (layer, 0, 3 * rw_blocks // 2)))

    args, specs = [z] * 8, []
    for block_of_step in (first_block, next_block):
        specs += [zspec("rr", block_of_step), zspec("kr", block_of_step), zspec("vr", block_of_step),
                  zspec("lora", block_of_step, 2 * LANES)]
    for a, s in (pair_row(mu, 0), pair_row(mu, rw_blocks), pair_row(mu, 2 * rw_blocks), mu_lora,
                 pair_row(params["w0"]), pair_cols(params["w_up"]),
                 pair_row(params["a0"]), pair_cols(params["a_up"]), pair_cols(params["g_up"]),
                 pair_row(params["k_k"]), pair_row(params["k_a"]), pair_row(params["r_k"]),
                 pair_row(params["ln_g"]), pair_row(params["ln_b"])):
        args.append(a)
        specs.append(s)

    n_chunks = tb // RWKV_CHUNK
    blk = pltpu.VMEM((tb, LANES), F32)
    ring = pltpu.VMEM((3, tb, LANES), F32)
    mat = lambda dtype: pltpu.VMEM((n_chunks, LANES, LANES), dtype)
    return pl.pallas_call(
        functools.partial(_rwkv_kernel, tb=tb),
        grid=(batch, n_pairs, nt + 1),
        in_specs=specs,
        out_specs=pl.BlockSpec((tb, LANES), lambda b, p, t: (b * nt + jnp.maximum(t - 1, 0), p)),
        out_shape=jax.ShapeDtypeStruct((batch * seq, n_pairs * LANES), BF16),
        scratch_shapes=[pltpu.VMEM((LANES, LANES), F32),
                        pltpu.VMEM((SUBLANES, LANES), F32),
                        pltpu.VMEM((SUBLANES, 2 * LANES), F32),
                        blk, blk, blk, blk, blk, blk,
                        ring, ring,
                        mat(BF16), mat(F32), mat(BF16), mat(F32),
                        pltpu.VMEM((SUBLANES * n_chunks, LANES), F32),
                        blk],
        compiler_params=pltpu.CompilerParams(
            dimension_semantics=("parallel", "parallel", "arbitrary"),
            vmem_limit_bytes=VMEM_LIMIT),
        name="rwkv7",
    )(*args)


def _proj_swa_kernel(sink_ref, h_ref, g_ref, w_ref, b_ref, o_ref, carry_ref, *,
                     n_kv, blocks_per_seq, layer_sink_base):
    starts_sequence = pl.program_id(0) % blocks_per_seq == 0

    @pl.when(starts_sequence)
    def _():
        carry_ref[...] = jnp.zeros_like(carry_ref)

    n = _rms_norm(h_ref[...], g_ref[...]).astype(BF16)
    tm = n.shape[0]
    w = SWA_WINDOW
    n_windows = tm // w
    m0 = _head_mask(w)
    keep_head0 = jnp.where(m0, 1.0, 0.0).astype(BF16)
    keep_head1 = jnp.where(m0, 0.0, 1.0).astype(BF16)
    from_prev = _iota2((PAIR * w, w), 1) > _iota2((PAIR * w, w), 0) % w
    first_bias = jnp.where(starts_sequence, -jnp.inf, 0.0).astype(F32)
    second_head = _iota2((PAIR * w, 1), 0) >= w
    head_cols = 4 * LANES

    def project_head(hd):
        parts = []
        for i in range(2):
            cols = slice(hd * head_cols + 2 * i * LANES, hd * head_cols + 2 * (i + 1) * LANES)
            z = jnp.dot(n, w_ref[:, cols], preferred_element_type=F32) + b_ref[:, cols]
            parts.append(z.astype(BF16))
            yield
        return parts

    def attend(hd, win, j, q4, kv):
        qb = 2 * hd + j
        rows = slice(win * w, (win + 1) * w)
        if win == 0:
            k_prev, v_prev = carry_ref[:, 2 * hd * LANES:(2 * hd + 1) * LANES], carry_ref[:, (2 * hd + 1) * LANES:(2 * hd + 2) * LANES]
            prev_bias = first_bias
        else:
            before = slice((win - 1) * w, win * w)
            k_prev, v_prev = kv[before, :LANES], kv[before, LANES:]
            prev_bias = 0.0
        q = q4[rows, j * LANES:(j + 1) * LANES] * (HEAD_DIM ** -0.5)
        kd = jnp.concatenate([k_prev, kv[rows, :LANES]], axis=0)
        vd = jnp.concatenate([v_prev, kv[rows, LANES:]], axis=0)
        q_stacked = jnp.concatenate([q * keep_head0, q * keep_head1], axis=0)
        s = _dot_nt(q_stacked, kd)
        yield
        sm = jnp.where(from_prev, s[:, :w] + prev_bias, s[:, w:])
        sink = jnp.where(second_head, sink_ref[layer_sink_base + 2 * qb + 1],
                         sink_ref[layer_sink_base + 2 * qb])
        m = jnp.maximum(jnp.max(sm, axis=-1, keepdims=True), sink)
        e = jnp.exp(sm - m)
        denom = jnp.sum(e, axis=-1, keepdims=True) + jnp.exp(sink - m)
        e_split = jnp.concatenate([jnp.where(from_prev, e, 0.0).astype(BF16),
                                   jnp.where(from_prev, 0.0, e).astype(BF16)], axis=1)
        pv = _dot(e_split, vd)
        yield
        o_ref[rows, qb * LANES:(qb + 1) * LANES] = _unstack_heads(pv / denom, m0).astype(o_ref.dtype)

    projected = _lockstep([project_head(0)])[0]
    for hd in range(n_kv):
        work = [attend(hd, win, j, *projected) for win in range(n_windows) for j in range(2)]
        if hd + 1 < n_kv:
            work.append(project_head(hd + 1))
        done = _lockstep(work)
        carry_ref[:, 2 * hd * LANES:(2 * hd + 2) * LANES] = projected[1][tm - w:tm, :]
        projected = done[-1]


def _proj_swa(h2, g_all, layer, w_all, b_all, sub, sinks_flat, seq, n_heads=16, n_kv=4, tm=1024):
    m, d = h2.shape
    qw = n_heads * HEAD_DIM
    blocks_per_seq = seq // tm
    return pl.pallas_call(
        functools.partial(_proj_swa_kernel, n_kv=n_kv, blocks_per_seq=blocks_per_seq,
                          layer_sink_base=sub * n_heads),
        grid=(m // tm,),
        in_specs=[pl.BlockSpec(memory_space=pltpu.SMEM),
                  pl.BlockSpec((tm, d), lambda i: (i, 0)), _layer_spec(g_all, layer),
                  _layer_spec(w_all, sub), _layer_spec(b_all, sub)],
        out_specs=pl.BlockSpec((tm, qw), lambda i: (i, 0)),
        out_shape=jax.ShapeDtypeStruct((m, qw), BF16),
        scratch_shapes=[pltpu.VMEM((SWA_WINDOW, n_kv * PAIR * LANES), BF16)],
        compiler_params=pltpu.CompilerParams(dimension_semantics=("arbitrary",),
                                             vmem_limit_bytes=VMEM_LIMIT),
        name="proj_swa",
    )(sinks_flat, h2, g_all, w_all, b_all)


def _post_kernel(*refs, n_mix, has_bias, has_final):
    it = iter(refs)
    h_ref = next(it)
    mix_refs = [next(it) for _ in range(n_mix)]
    wo_refs = [next(it) for _ in range(n_mix)]
    bo_ref = next(it) if has_bias else None
    g2_ref, wg_ref, wu_ref, wd_ref = next(it), next(it), next(it), next(it)
    gf_ref = next(it) if has_final else None
    o_ref = next(it)

    h = h_ref[...]
    for m_ref, w_ref in zip(mix_refs, wo_refs):
        h = h + jnp.dot(m_ref[...].astype(BF16), w_ref[...], preferred_element_type=F32)
    if has_bias:
        h = h + bo_ref[...]
    n = _rms_norm(h, g2_ref[...]).astype(BF16)
    gate = jnp.dot(n, wg_ref[...], preferred_element_type=F32)
    up = jnp.dot(n, wu_ref[...], preferred_element_type=F32)
    act = (gate * jax.nn.sigmoid(gate) * up).astype(BF16)
    h = h + jnp.dot(act, wd_ref[...], preferred_element_type=F32)
    if has_final:
        h = _rms_norm(h, gf_ref[...])
    o_ref[...] = h


def _post(h2, mixes, wo_all, sub, bo_all, g2_all, wg_all, wu_all, wd_all, layer, final_g, tm=512):
    m, d = h2.shape
    row = lambda width: pl.BlockSpec((tm, width), lambda i: (i, 0))
    args, specs = [h2], [row(d)]
    for mx in mixes:
        args.append(mx)
        specs.append(row(mx.shape[1]))
    for k, mx in enumerate(mixes):
        args.append(wo_all)
        specs.append(_layer_spec(wo_all, sub, rows=mx.shape[1], row_block=k))
    if bo_all is not None:
        args.append(bo_all)
        specs.append(_layer_spec(bo_all, sub))
    for a in (g2_all, wg_all, wu_all, wd_all):
        args.append(a)
        specs.append(_layer_spec(a, layer))
    if final_g is not None:
        args.append(final_g.reshape(1, d))
        specs.append(pl.BlockSpec((1, d), lambda i: (0, 0), pipeline_mode=pl.Buffered(1)))
    bo = bo_all
    return pl.pallas_call(
        functools.partial(_post_kernel, n_mix=len(mixes), has_bias=bo is not None,
                          has_final=final_g is not None),
        grid=(m // tm,),
        in_specs=specs,
        out_specs=row(d),
        out_shape=jax.ShapeDtypeStruct((m, d), F32),
        compiler_params=pltpu.CompilerParams(dimension_semantics=("parallel",),
                                             vmem_limit_bytes=VMEM_LIMIT),
        name="post_ffn",
    )(*args)


def _even_layout(w_in, w_up, a_up):
    n_layers, d, _ = w_in.shape
    rw = GROUP_WIDTH

    def interleave_halves(w):
        w6 = w.reshape(n_layers, d, rw // LANES, PAIR, 2, HEAD_DIM // 2)
        return w6.transpose(0, 1, 2, 4, 3, 5).reshape(n_layers, d, rw)

    q, k = interleave_halves(w_in[:, :, :rw]), interleave_halves(w_in[:, :, rw:2 * rw])
    v, g = w_in[:, :, 2 * rw:3 * rw], w_in[:, :, 3 * rw:4 * rw]
    per_pair = [x[:, :, p * LANES:(p + 1) * LANES] for p in range(rw // LANES) for x in (q, k, v, g)]
    w = jnp.concatenate(per_pair + [w_in[:, :, 4 * rw:]], axis=2).astype(BF16)
    names = ["rr", "kr", "vr", "lora"]
    col = {nm: i * (rw // LANES) for i, nm in enumerate(names)}
    zeros = jnp.zeros_like(w_up)
    w_up_pad = jnp.concatenate([w_up, zeros], axis=1).astype(BF16)
    a_up_pad = jnp.concatenate([zeros, a_up], axis=1).astype(BF16)
    return w, col, w_up_pad, a_up_pad


def _swa_layout(w_qkv, b_qkv, n_heads=16, n_kv=4):
    qw = n_heads * HEAD_DIM
    kw = n_kv * HEAD_DIM
    group = qw // n_kv

    def layout(x):
        q, k, v = x[..., :qw], x[..., qw:qw + kw], x[..., qw + kw:]
        parts = []
        for hd in range(n_kv):
            k_h = k[..., hd * HEAD_DIM:(hd + 1) * HEAD_DIM]
            v_h = v[..., hd * HEAD_DIM:(hd + 1) * HEAD_DIM]
            parts += [q[..., hd * group:(hd + 1) * group], k_h, k_h, v_h, v_h]
        return jnp.concatenate(parts, axis=-1)

    return layout(w_qkv).astype(BF16), layout(b_qkv)[:, None, :]


def _rope_tables(seq, dtype):
    half = HEAD_DIM // 2
    inv_freq = ROPE_BASE ** (-jnp.linspace(0.0, 1.0, half, dtype=dtype))
    ang = jnp.arange(seq, dtype=dtype)[:, None] * inv_freq[None, :]
    sin = jnp.sin(ang)
    cos = jnp.cos(ang)
    return (jnp.tile(cos, (1, LANES // half)),
            jnp.concatenate([-sin] * PAIR + [sin] * PAIR, axis=1))


def kernel(x, norm1_g, norm2_g, final_g, even_w_in, even_w_out, rwkv_mu, rwkv_w0, rwkv_w_up,
           rwkv_a0, rwkv_a_up, rwkv_g_up, rwkv_k_k, rwkv_k_a, rwkv_r_k, rwkv_ln_g, rwkv_ln_b,
           swa_w_qkv, swa_b_qkv, swa_sinks, swa_w_o, swa_b_o, ffn_w_gate, ffn_w_up, ffn_w_down):
    batch, seq, d = x.shape
    depth = norm1_g.shape[0]
    h = x.reshape(batch * seq, d)
    cos, sin = _rope_tables(seq, x.dtype)
    sinks_flat = swa_sinks.reshape(-1)

    def rows(p):
        return p.reshape(p.shape[0], 1, -1)

    even_w, col, w_up_pad, a_up_pad = _even_layout(even_w_in, rwkv_w_up, rwkv_a_up)
    rwkv_params = dict(mu=rows(rwkv_mu), w0=rows(rwkv_w0), w_up=w_up_pad, a0=rows(rwkv_a0), a_up=a_up_pad,
                       g_up=rwkv_g_up.astype(BF16), k_k=rows(rwkv_k_k), k_a=rows(rwkv_k_a),
                       r_k=rows(rwkv_r_k), ln_g=rows(rwkv_ln_g), ln_b=rows(rwkv_ln_b))
    swa_w, swa_b = _swa_layout(swa_w_qkv, swa_b_qkv)
    g1, g2 = rows(norm1_g), rows(norm2_g)
    even_wo, swa_wo, swa_bo = even_w_out.astype(BF16), swa_w_o.astype(BF16), rows(swa_b_o)
    wg, wu, wd = ffn_w_gate.astype(BF16), ffn_w_up.astype(BF16), ffn_w_down.astype(BF16)

    for layer in range(depth):
        i = layer // 2
        if layer % 2 == 0:
            z, ret_out = _proj_retention(h, g1, layer, even_w, i, cos, sin, seq)
            mixes = [ret_out, _rwkv(z, rwkv_params, i, batch, seq, col)]
            wo, bo = even_wo, None
        else:
            mixes = [_proj_swa(h, g1, layer, swa_w, swa_b, i, sinks_flat, seq)]
            wo, bo = swa_wo, swa_bo
        h = _post(h, mixes, wo, i, bo, g2, wg, wu, wd, layer, final_g if layer == depth - 1 else None)
    return h.reshape(batch, seq, d)
```

```python
import functools

import jax
import jax.numpy as jnp
from jax import lax
from jax.experimental import pallas as pl
from jax.experimental.pallas import tpu as pltpu

F32 = jnp.float32
BF16 = jnp.bfloat16

LANES = 128
SUBLANES = 8
HEAD_DIM = 64
PAIR = LANES // HEAD_DIM
GROUP_WIDTH = 8 * HEAD_DIM
RMS_EPS = 1e-6
RET_GN_EPS = 1e-6
RWKV_GN_EPS = 64e-5
ROPE_BASE = 10000.0
RET_CHUNK = 128
RWKV_CHUNK = 64
SWA_WINDOW = 128
VMEM_LIMIT = 56 * 1024 * 1024


def _dot(a, b):
    return jnp.dot(a.astype(BF16), b.astype(BF16), preferred_element_type=F32)


def _dot_nt(a, b):
    return lax.dot_general(a.astype(BF16), b.astype(BF16), (((1,), (1,)), ((), ())),
                           preferred_element_type=F32)


def _split2(x):
    hi = x.astype(BF16)
    lo = (x - hi.astype(F32)).astype(BF16)
    return hi, lo


def _dot_hi2(x, m):
    hi, lo = _split2(x)
    return (jnp.dot(hi, m, preferred_element_type=F32)
            + jnp.dot(lo, m, preferred_element_type=F32))


def _dot_hi3_left(m, x):
    hi = x.astype(BF16)
    r1 = x - hi.astype(F32)
    mid = r1.astype(BF16)
    lo = (r1 - mid.astype(F32)).astype(BF16)
    return (jnp.dot(m, hi, preferred_element_type=F32)
            + jnp.dot(m, mid, preferred_element_type=F32)
            + jnp.dot(m, lo, preferred_element_type=F32))


def _iota2(shape, dim):
    return lax.broadcasted_iota(jnp.int32, shape, dim)


def _head_mask(rows):
    return _iota2((rows, LANES), 1) < HEAD_DIM


def _stack_heads(x, m0):
    zero = jnp.zeros_like(x)
    return jnp.concatenate([jnp.where(m0, x, zero), jnp.where(m0, zero, x)], axis=0)


def _unstack_heads(xs, m0):
    r = xs.shape[0] // 2
    return jnp.where(m0, xs[:r], xs[r:])


def _group_matrix(scale):
    r = _iota2((LANES, LANES), 0) // HEAD_DIM
    c = _iota2((LANES, LANES), 1) // HEAD_DIM
    return jnp.where(r == c, scale, 0.0).astype(BF16)


def _rms_norm(x, g):
    ms = jnp.mean(x * x, axis=-1, keepdims=True)
    return x * lax.rsqrt(ms + RMS_EPS) * g


def _layer_spec(x, layer, rows=None, row_block=0):
    _, r, c = x.shape
    return pl.BlockSpec((None, rows or r, c), lambda *_: (layer, row_block, 0),
                        pipeline_mode=pl.Buffered(1))


def _proj_retention_kernel(h_ref, g_ref, w_ref, cos_ref, sin_ref, dmask_ref, qdec_ref, kdec_ref, cdec_ref,
                           z_ref, o_ref, s_ref, *, n_pairs, blocks_per_seq, rest_stage_cols):
    @pl.when(pl.program_id(0) % blocks_per_seq == 0)
    def _():
        s_ref[...] = jnp.zeros_like(s_ref)

    n = _rms_norm(h_ref[...], g_ref[...]).astype(BF16)
    tm = n.shape[0]
    c = RET_CHUNK
    n_chunks = tm // c
    half = HEAD_DIM // 2
    m0 = _head_mask(c)
    m0_qk = (_iota2((c, LANES), 1) // half) % PAIR == 0
    gmean = _group_matrix(1.0 / HEAD_DIM)
    key_head = (_iota2((LANES, LANES), 0) // half) % PAIR
    value_head = _iota2((LANES, LANES), 1) // HEAD_DIM
    same_head = key_head == value_head
    pair_cols = 4 * LANES

    def project(c0, c1):
        return jnp.dot(n, w_ref[:, c0:c1], preferred_element_type=F32)

    def project_pair(p):
        parts = []
        for i in range(2):
            both = project(p * pair_cols + 2 * i * LANES, p * pair_cols + 2 * (i + 1) * LANES)
            parts += [both[:, :LANES], both[:, LANES:]]
            yield
        return parts

    def project_rest():
        base = n_pairs * pair_cols
        for c0 in range(0, z_ref.shape[1], rest_stage_cols):
            z_ref[:, c0:c0 + rest_stage_cols] = project(base + c0, base + c0 + rest_stage_cols)
            yield

    def swap_halves(x):
        return pltpu.roll(x, HEAD_DIM, axis=1)

    def retention_pair(p, zq, zk, zv, zg):
        dmask, qdec, kdec, cdec = dmask_ref[p], qdec_ref[p], kdec_ref[p], cdec_ref[p]

        def inner_terms(j):
            rows = slice(j * c, (j + 1) * c)
            cos = cos_ref[rows, :]
            sin = sin_ref[rows, :]
            q = zq[rows] * cos + swap_halves(zq[rows]) * sin
            k = (zk[rows] * cos + swap_halves(zk[rows]) * sin) * (HEAD_DIM ** -0.5)
            v = zv[rows].astype(BF16)
            scores = _dot_nt(_stack_heads(q, m0_qk), k) * dmask
            kv = jnp.where(same_head, _dot((k * kdec).T, v), 0.0)
            yield
            o_inner = _unstack_heads(_dot(scores, v), m0)
            return (q * qdec).astype(BF16), o_inner, kv

        inner = yield from _lockstep_stages(inner_terms(j) for j in range(n_chunks))
        state = s_ref[p]
        states = []
        for _, _, kv in inner:
            states.append(state)
            state = cdec * state + kv
        s_ref[p] = state

        def finish(j):
            rows = slice(j * c, (j + 1) * c)
            q_dec, o_inner, _ = inner[j]
            o = o_inner + _dot(q_dec, states[j])
            yield
            mu = _dot_hi2(o, gmean)
            yield
            d = o - mu
            var = _dot_hi2(d * d, gmean)
            yield
            g = zg[rows]
            out = d * lax.rsqrt(var + RET_GN_EPS) * (g * jax.nn.sigmoid(g))
            o_ref[rows, p * LANES:(p + 1) * LANES] = out.astype(o_ref.dtype)

        yield from _lockstep_stages(finish(j) for j in range(n_chunks))

    rest = project_rest()
    n_rest_stages = z_ref.shape[1] // rest_stage_cols

    def upcoming(p):
        parts = None
        if p + 1 < n_pairs:
            parts = yield from project_pair(p + 1)
        first = n_rest_stages * p // n_pairs
        last = n_rest_stages * (p + 1) // n_pairs
        for _ in range(first, last):
            next(rest)
            yield
        return parts

    projected = _lockstep([project_pair(0)])[0]
    for p in range(n_pairs):
        projected = _lockstep([retention_pair(p, *projected), upcoming(p)])[1]


def _retention_tables(dtype):
    c = RET_CHUNK
    n_heads = GROUP_WIDTH // HEAD_DIM
    h = jnp.arange(n_heads, dtype=dtype)
    log_gamma = jnp.log1p(-(2.0 ** (-5.0 - h)))
    idx = jnp.arange(c, dtype=dtype)
    rel = idx[:, None] - idx[None, :]
    inner = jnp.where(rel >= 0, jnp.exp(jnp.maximum(rel, 0.0)[None] * log_gamma[:, None, None]), 0.0)
    dmask = inner.reshape(n_heads // PAIR, PAIR * c, c)
    pair_lg = log_gamma.reshape(n_heads // PAIR, PAIR)
    lanes_lg = jnp.repeat(pair_lg, HEAD_DIM, axis=1)
    qk_lg = jnp.tile(jnp.repeat(pair_lg, HEAD_DIM // 2, axis=1), (1, 2))
    qdec = jnp.exp((idx + 1.0)[None, :, None] * qk_lg[:, None, :])
    kdec = jnp.exp((c - 1 - idx)[None, :, None] * qk_lg[:, None, :])
    cdec = jnp.broadcast_to(jnp.exp(c * lanes_lg)[:, None, :], (n_heads // PAIR, LANES, LANES))
    return dmask, qdec, kdec, cdec


def _proj_retention(h2, g_all, layer, w_all, sub, cos, sin, seq, n_pairs=4, tm=1024):
    m, d = h2.shape
    n_rest = w_all.shape[2] - n_pairs * 4 * LANES
    blocks_per_seq = seq // tm
    tables = _retention_tables(h2.dtype)
    whole = lambda x: pl.BlockSpec(x.shape, lambda i: (0,) * x.ndim, pipeline_mode=pl.Buffered(1))
    rope = pl.BlockSpec((tm, LANES), lambda i: (i % blocks_per_seq, 0))
    return pl.pallas_call(
        functools.partial(_proj_retention_kernel, n_pairs=n_pairs, blocks_per_seq=blocks_per_seq,
                          rest_stage_cols=2 * LANES),
        grid=(m // tm,),
        in_specs=[pl.BlockSpec((tm, d), lambda i: (i, 0)), _layer_spec(g_all, layer), _layer_spec(w_all, sub),
                  rope, rope] + [whole(t) for t in tables],
        out_specs=[pl.BlockSpec((tm, n_rest), lambda i: (i, 0)),
                   pl.BlockSpec((tm, n_pairs * LANES), lambda i: (i, 0))],
        out_shape=[jax.ShapeDtypeStruct((m, n_rest), F32),
                   jax.ShapeDtypeStruct((m, n_pairs * LANES), BF16)],
        scratch_shapes=[pltpu.VMEM((n_pairs, LANES, LANES), F32)],
        compiler_params=pltpu.CompilerParams(dimension_semantics=("arbitrary",),
                                             vmem_limit_bytes=VMEM_LIMIT),
        name="proj_retention",
    )(h2, g_all, w_all, cos, sin, *tables)


def _lockstep_stages(gens):
    gens = list(gens)
    results = [None] * len(gens)
    live = list(range(len(gens)))
    while live:
        still = []
        for i in live:
            try:
                next(gens[i])
                still.append(i)
            except StopIteration as stop:
                results[i] = stop.value
        live = still
        if live:
            yield
    return results


def _lockstep(gens):
    stages = _lockstep_stages(gens)
    while True:
        try:
            next(stages)
        except StopIteration as stop:
            return stop.value


def _inverse_masks():
    r = _iota2((LANES, LANES), 0)
    c = _iota2((LANES, LANES), 1)
    eye = jnp.where(r == c, 1.0, 0.0).astype(F32)
    diag8 = r // 8 == c // 8
    lower_left = [(r // (2 * k) == c // (2 * k)) & (r // k > c // k) for k in (8, 16, 32)]
    return eye, diag8, lower_left


def _unit_lower_inverse(n_mat, masks):
    eye, diag8, lower_left = masks
    p = jnp.where(diag8, n_mat, 0.0)
    t = eye + p
    p = _dot(p, p)
    yield
    tp = _dot(jnp.concatenate([t, p], axis=0), p)
    t = t + tp[:LANES]
    p = tp[LANES:]
    yield
    t = t + _dot(t, p)
    yield
    for mask in lower_left:
        lt = _dot(jnp.where(mask, n_mat, 0.0), t)
        yield
        t = t + _dot(t, lt)
        yield
    return t


def _delayed(gen, rounds):
    for _ in range(rounds):
        yield
    yield from gen


def _rwkv_kernel(fr_ref, fk_ref, fv_ref, fl_ref, nr_ref, nk_ref, nv_ref, nl_ref,
                 mur_ref, muk_ref, muv_ref, mul_ref,
                 w0_ref, wup_ref, a0_ref, aup_ref, gup_ref, kk_ref, ka_ref, rk_ref, lng_ref, lnb_ref,
                 o_ref,
                 ht_ref, prev_ref, prevl_ref, r_s, lw_s, k_s, v_s, a_s, b_s, bonus_s, gate_s,
                 rhat_s, yhat_s, m_s, g_s, gtot_s, y_s, *, tb):
    t_idx = pl.program_id(2)
    c = RWKV_CHUNK
    n_chunks = tb // c
    gsum = _group_matrix(1.0)
    gmean = _group_matrix(1.0 / HEAD_DIM)
    pr = 2 * c
    first_row = _iota2((pr, 1), 0) == 0

    def prepare(zr_ref, zk_ref, zv_ref, zl_ref, slot):
        def prepare_rows(j):
            rows = slice(j * pr, (j + 1) * pr)

            def mixed(z_ref, mu_ref, carry_ref, carry_row):
                z = z_ref[rows, :]
                before = carry_ref[carry_row:carry_row + 1, :] if j == 0 else z_ref[j * pr - 1:j * pr, :]
                z_prev = jnp.where(first_row, before, pltpu.roll(z, 1, axis=0))
                return z + mu_ref[...] * (z_prev - z)

            xr = mixed(zr_ref, mur_ref, prev_ref, 0)
            xk = mixed(zk_ref, muk_ref, prev_ref, 1)
            xv = mixed(zv_ref, muv_ref, prev_ref, 2)
            xl = mixed(zl_ref, mul_ref, prevl_ref, 0)
            x_wa = xl[:, :LANES]
            w_pre = _dot(jnp.tanh(x_wa), wup_ref[...])
            a_pre = _dot(x_wa, aup_ref[...])
            gate = _dot(jax.nn.sigmoid(xl[:, LANES:]), gup_ref[...])
            kkf = xk * kk_ref[...]
            norm2 = _dot_hi2(kkf * kkf, gsum)
            yield
            w_log = -jax.nn.softplus(-(w0_ref[...] + w_pre)) - 0.5
            a = jax.nn.sigmoid(a0_ref[...] + a_pre)
            kk = kkf / jnp.maximum(jnp.sqrt(norm2), 1e-12)
            k2 = xk * (1.0 + (a - 1.0) * ka_ref[...])
            rk_sum = _dot_hi2(xr * k2 * rk_ref[...], gsum)
            gate_s[slot, rows, :] = gate
            r_s[rows, :] = xr
            lw_s[rows, :] = -jnp.exp(w_log)
            k_s[rows, :] = k2
            v_s[rows, :] = xv
            a_s[rows, :] = -kk
            b_s[rows, :] = kk * a
            yield
            bonus_s[slot, rows, :] = rk_sum * xv

        waiting = [prepare_rows(j) for j in range(tb // pr)]
        running = []
        while waiting or running:
            if waiting:
                running.append(waiting.pop(0))
            running = [g for g in running if next(g, "done") != "done"]
            yield
        prev_ref[0:1, :] = zr_ref[tb - 1:tb, :]
        prev_ref[1:2, :] = zk_ref[tb - 1:tb, :]
        prev_ref[2:3, :] = zv_ref[tb - 1:tb, :]
        prevl_ref[0:1, :] = zl_ref[tb - 1:tb, :]

    @pl.when(t_idx == 0)
    def _():
        for ref in (ht_ref, prev_ref, prevl_ref, bonus_s, gate_s, rhat_s, yhat_s, m_s, g_s, gtot_s):
            ref[...] = jnp.zeros_like(ref)
        for _ in prepare(fr_ref, fk_ref, fv_ref, fl_ref, 0):
            pass

    slot_next = lax.rem(t_idx + 1, 3)
    slot_done = lax.rem(t_idx + 2, 3)

    m0 = _head_mask(c)
    ri = _iota2((LANES, LANES), 0)
    ci = _iota2((LANES, LANES), 1)
    same_head = ri // c == ci // c
    strict = same_head & (ri > ci)
    incl = same_head & (ri >= ci)
    tri = jnp.where(_iota2((c, c), 0) >= _iota2((c, c), 1), 1.0, 0.0).astype(BF16)

    inv_masks = _inverse_masks()

    def chunk_terms(j):
        rows = slice(j * c, (j + 1) * c)
        lw = lw_s[rows, :]
        cum = _dot_hi3_left(tri, lw)
        yield
        cum_last = cum[c - 1:c, :]
        g_inv = jnp.exp(-cum)
        g_rem = jnp.exp(cum_last - cum)
        a_st = _stack_heads(a_s[rows, :] * jnp.exp(cum - lw), m0)
        r_st = _stack_heads(r_s[rows, :] * jnp.exp(cum), m0)
        v_st = _stack_heads(v_s[rows, :], m0)
        bg_st = _stack_heads(b_s[rows, :] * g_rem, m0)
        kg_st = _stack_heads(k_s[rows, :] * g_rem, m0)
        b_t = (b_s[rows, :] * g_inv).astype(BF16)
        k_t = (k_s[rows, :] * g_inv).astype(BF16)
        b2 = jnp.concatenate([b_t, b_t], axis=0)
        k2_ = jnp.concatenate([k_t, k_t], axis=0)
        ar_bf = jnp.concatenate([a_st, r_st], axis=0).astype(BF16)
        v_bf = v_st.astype(BF16)
        ar_b = _dot_nt(ar_bf, b2)
        ar_k = _dot_nt(ar_bf, k2_)
        n_ab = jnp.where(strict, ar_b[:LANES], 0.0)
        n_ak = jnp.where(strict, ar_k[:LANES], 0.0)
        n_rb = jnp.where(incl, ar_b[LANES:], 0.0)
        n_rk = jnp.where(incl, ar_k[LANES:], 0.0)
        yield
        akv = _dot(n_ak, v_bf)
        bk_t = jnp.concatenate([bg_st, kg_st], axis=0).T.astype(BF16)
        t_inv = yield from _unit_lower_inverse(n_ab, inv_masks)
        wu = _dot(t_inv, jnp.concatenate([a_st, akv], axis=1))
        yield
        wu_v = jnp.concatenate(
            [wu.astype(BF16), jnp.concatenate([jnp.zeros_like(v_bf), v_bf], axis=1)], axis=0)
        lhs = jnp.concatenate([jnp.concatenate([n_rb, n_rk], axis=1).astype(BF16), bk_t], axis=0)
        both = jnp.dot(lhs, wu_v, preferred_element_type=F32)
        rw = both[:LANES]
        mg_t = both[LANES:]
        yield
        return ((r_st + rw[:, :LANES]).astype(BF16), rw[:, LANES:], mg_t[:, :LANES].astype(BF16),
                mg_t[:, LANES:].T, jnp.exp(cum_last))

    def chain_and_output():
        ht = ht_ref[...]
        for j in range(n_chunks):
            y_st = _dot_nt(rhat_s[j], ht) + yhat_s[j]
            y_s[j * c:(j + 1) * c, :] = y_st[:c] + y_st[c:]
            ht = ht * gtot_s[SUBLANES * j:SUBLANES * j + 1, :] + _dot_nt(ht, m_s[j]) + g_s[j]
            yield
        ht_ref[...] = ht
        y = y_s[...]
        mu = _dot_hi2(y, gmean)
        yield
        d = y - mu
        var = _dot_hi2(d * d, gmean)
        yield
        yn = d * lax.rsqrt(var + RWKV_GN_EPS) * lng_ref[...] + lnb_ref[...]
        o_ref[...] = ((yn + bonus_s[slot_done]) * gate_s[slot_done]).astype(o_ref.dtype)

    done = _lockstep([chunk_terms(j) for j in range(n_chunks)]
                     + [chain_and_output(), _delayed(prepare(nr_ref, nk_ref, nv_ref, nl_ref, slot_next), 2)])
    for j, (r_hat, y_hat, m_t, g_mat, g_tot) in enumerate(done[:n_chunks]):
        rhat_s[j] = r_hat
        yhat_s[j] = y_hat
        m_s[j] = m_t
        g_s[j] = g_mat
        gtot_s[SUBLANES * j:SUBLANES * j + 1, :] = g_tot


def _rwkv(z, params, layer, batch, seq, col, tb=1024):
    nt = seq // tb
    n_pairs = 4

    def zspec(name, block_of_step, width=LANES):
        off = col[name]
        if width == LANES:
            return pl.BlockSpec((tb, LANES), lambda b, p, t: (b * nt + block_of_step(t), off + p))
        return pl.BlockSpec((tb, width), lambda b, p, t: (b * nt + block_of_step(t), off * LANES // width))

    first_block = lambda t: 0
    next_block = lambda t: jnp.minimum(t + 1, nt - 1)

    def pair_row(x, first_block=0):
        return x, pl.BlockSpec((None, 1, LANES), lambda b, p, t: (layer, 0, first_block + p))

    def pair_cols(x):
        return x, pl.BlockSpec((None, x.shape[1], LANES), lambda b, p, t: (layer, 0, p))

    mu = params["mu"]
    rw_blocks = GROUP_WIDTH // LANES
    mu_lora = (mu, pl.BlockSpec((None, 1, 2 * LANES), lambda b, p, t: (layer, 0, 3 * rw_blocks // 2)))

    args, specs = [z] * 8, []
    for block_of_step in (first_block, next_block):
        specs += [zspec("rr", block_of_step), zspec("kr", block_of_step), zspec("vr", block_of_step),
                  zspec("lora", block_of_step, 2 * LANES)]
    for a, s in (pair_row(mu, 0), pair_row(mu, rw_blocks), pair_row(mu, 2 * rw_blocks), mu_lora,
                 pair_row(params["w0"]), pair_cols(params["w_up"]),
                 pair_row(params["a0"]), pair_cols(params["a_up"]), pair_cols(params["g_up"]),
                 pair_row(params["k_k"]), pair_row(params["k_a"]), pair_row(params["r_k"]),
                 pair_row(params["ln_g"]), pair_row(params["ln_b"])):
        args.append(a)
        specs.append(s)

    n_chunks = tb // RWKV_CHUNK
    blk = pltpu.VMEM((tb, LANES), F32)
    ring = pltpu.VMEM((3, tb, LANES), F32)
    mat = lambda dtype: pltpu.VMEM((n_chunks, LANES, LANES), dtype)
    return pl.pallas_call(
        functools.partial(_rwkv_kernel, tb=tb),
        grid=(batch, n_pairs, nt + 1),
        in_specs=specs,
        out_specs=pl.BlockSpec((tb, LANES), lambda b, p, t: (b * nt + jnp.maximum(t - 1, 0), p)),
        out_shape=jax.ShapeDtypeStruct((batch * seq, n_pairs * LANES), BF16),
        scratch_shapes=[pltpu.VMEM((LANES, LANES), F32),
                        pltpu.VMEM((SUBLANES, LANES), F32),
                        pltpu.VMEM((SUBLANES, 2 * LANES), F32),
                        blk, blk, blk, blk, blk, blk,
                        ring, ring,
                        mat(BF16), mat(F32), mat(BF16), mat(F32),
                        pltpu.VMEM((SUBLANES * n_chunks, LANES), F32),
                        blk],
        compiler_params=pltpu.CompilerParams(
            dimension_semantics=("parallel", "parallel", "arbitrary"),
            vmem_limit_bytes=VMEM_LIMIT),
        name="rwkv7",
    )(*args)


def _proj_swa_kernel(sink_ref, h_ref, g_ref, w_ref, b_ref, o_ref, carry_ref, *,
                     n_kv, blocks_per_seq, layer_sink_base):
    starts_sequence = pl.program_id(0) % blocks_per_seq == 0

    @pl.when(starts_sequence)
    def _():
        carry_ref[...] = jnp.zeros_like(carry_ref)

    n = _rms_norm(h_ref[...], g_ref[...]).astype(BF16)
    tm = n.shape[0]
    w = SWA_WINDOW
    n_windows = tm // w
    m0 = _head_mask(w)
    keep_head0 = jnp.where(m0, 1.0, 0.0).astype(BF16)
    keep_head1 = jnp.where(m0, 0.0, 1.0).astype(BF16)
    from_prev = _iota2((PAIR * w, w), 1) > _iota2((PAIR * w, w), 0) % w
    first_bias = jnp.where(starts_sequence, -jnp.inf, 0.0).astype(F32)
    second_head = _iota2((PAIR * w, 1), 0) >= w
    head_cols = 4 * LANES

    def project_head(hd):
        parts = []
        for i in range(2):
            cols = slice(hd * head_cols + 2 * i * LANES, hd * head_cols + 2 * (i + 1) * LANES)
            z = jnp.dot(n, w_ref[:, cols], preferred_element_type=F32) + b_ref[:, cols]
            parts.append(z.astype(BF16))
            yield
        return parts

    def attend(hd, win, j, q4, kv):
        qb = 2 * hd + j
        rows = slice(win * w, (win + 1) * w)
        if win == 0:
            k_prev, v_prev = carry_ref[:, 2 * hd * LANES:(2 * hd + 1) * LANES], carry_ref[:, (2 * hd + 1) * LANES:(2 * hd + 2) * LANES]
            prev_bias = first_bias
        else:
            before = slice((win - 1) * w, win * w)
            k_prev, v_prev = kv[before, :LANES], kv[before, LANES:]
            prev_bias = 0.0
        q = q4[rows, j * LANES:(j + 1) * LANES] * (HEAD_DIM ** -0.5)
        kd = jnp.concatenate([k_prev, kv[rows, :LANES]], axis=0)
        vd = jnp.concatenate([v_prev, kv[rows, LANES:]], axis=0)
        q_stacked = jnp.concatenate([q * keep_head0, q * keep_head1], axis=0)
        s = _dot_nt(q_stacked, kd)
        yield
        sm = jnp.where(from_prev, s[:, :w] + prev_bias, s[:, w:])
        sink = jnp.where(second_head, sink_ref[layer_sink_base + 2 * qb + 1],
                         sink_ref[layer_sink_base + 2 * qb])
        m = jnp.maximum(jnp.max(sm, axis=-1, keepdims=True), sink)
        e = jnp.exp(sm - m)
        denom = jnp.sum(e, axis=-1, keepdims=True) + jnp.exp(sink - m)
        e_split = jnp.concatenate([jnp.where(from_prev, e, 0.0).astype(BF16),
                                   jnp.where(from_prev, 0.0, e).astype(BF16)], axis=1)
        pv = _dot(e_split, vd)
        yield
        o_ref[rows, qb * LANES:(qb + 1) * LANES] = _unstack_heads(pv / denom, m0).astype(o_ref.dtype)

    projected = _lockstep([project_head(0)])[0]
    for hd in range(n_kv):
        work = [attend(hd, win, j, *projected) for win in range(n_windows) for j in range(2)]
        if hd + 1 < n_kv:
            work.append(project_head(hd + 1))
        done = _lockstep(work)
        carry_ref[:, 2 * hd * LANES:(2 * hd + 2) * LANES] = projected[1][tm - w:tm, :]
        projected = done[-1]


def _proj_swa(h2, g_all, layer, w_all, b_all, sub, sinks_flat, seq, n_heads=16, n_kv=4, tm=1024):
    m, d = h2.shape
    qw = n_heads * HEAD_DIM
    blocks_per_seq = seq // tm
    return pl.pallas_call(
        functools.partial(_proj_swa_kernel, n_kv=n_kv, blocks_per_seq=blocks_per_seq,
                          layer_sink_base=sub * n_heads),
        grid=(m // tm,),
        in_specs=[pl.BlockSpec(memory_space=pltpu.SMEM),
                  pl.BlockSpec((tm, d), lambda i: (i, 0)), _layer_spec(g_all, layer),
                  _layer_spec(w_all, sub), _layer_spec(b_all, sub)],
        out_specs=pl.BlockSpec((tm, qw), lambda i: (i, 0)),
        out_shape=jax.ShapeDtypeStruct((m, qw), BF16),
        scratch_shapes=[pltpu.VMEM((SWA_WINDOW, n_kv * PAIR * LANES), BF16)],
        compiler_params=pltpu.CompilerParams(dimension_semantics=("arbitrary",),
                                             vmem_limit_bytes=VMEM_LIMIT),
        name="proj_swa",
    )(sinks_flat, h2, g_all, w_all, b_all)


def _post_kernel(*refs, n_mix, has_bias, has_final):
    it = iter(refs)
    h_ref = next(it)
    mix_refs = [next(it) for _ in range(n_mix)]
    wo_refs = [next(it) for _ in range(n_mix)]
    bo_ref = next(it) if has_bias else None
    g2_ref, wg_ref, wu_ref, wd_ref = next(it), next(it), next(it), next(it)
    gf_ref = next(it) if has_final else None
    o_ref = next(it)

    h = h_ref[...]
    for m_ref, w_ref in zip(mix_refs, wo_refs):
        h = h + jnp.dot(m_ref[...].astype(BF16), w_ref[...], preferred_element_type=F32)
    if has_bias:
        h = h + bo_ref[...]
    n = _rms_norm(h, g2_ref[...]).astype(BF16)
    gate = jnp.dot(n, wg_ref[...], preferred_element_type=F32)
    up = jnp.dot(n, wu_ref[...], preferred_element_type=F32)
    act = (gate * jax.nn.sigmoid(gate) * up).astype(BF16)
    h = h + jnp.dot(act, wd_ref[...], preferred_element_type=F32)
    if has_final:
        h = _rms_norm(h, gf_ref[...])
    o_ref[...] = h


def _post(h2, mixes, wo_all, sub, bo_all, g2_all, wg_all, wu_all, wd_all, layer, final_g, tm=512):
    m, d = h2.shape
    row = lambda width: pl.BlockSpec((tm, width), lambda i: (i, 0))
    args, specs = [h2], [row(d)]
    for mx in mixes:
        args.append(mx)
        specs.append(row(mx.shape[1]))
    for k, mx in enumerate(mixes):
        args.append(wo_all)
        specs.append(_layer_spec(wo_all, sub, rows=mx.shape[1], row_block=k))
    if bo_all is not None:
        args.append(bo_all)
        specs.append(_layer_spec(bo_all, sub))
    for a in (g2_all, wg_all, wu_all, wd_all):
        args.append(a)
        specs.append(_layer_spec(a, layer))
    if final_g is not None:
        args.append(final_g.reshape(1, d))
        specs.append(pl.BlockSpec((1, d), lambda i: (0, 0), pipeline_mode=pl.Buffered(1)))
    bo = bo_all
    return pl.pallas_call(
        functools.partial(_post_kernel, n_mix=len(mixes), has_bias=bo is not None,
                          has_final=final_g is not None),
        grid=(m // tm,),
        in_specs=specs,
        out_specs=row(d),
        out_shape=jax.ShapeDtypeStruct((m, d), F32),
        compiler_params=pltpu.CompilerParams(dimension_semantics=("parallel",),
                                             vmem_limit_bytes=VMEM_LIMIT),
        name="post_ffn",
    )(*args)


def _even_layout(w_in, w_up, a_up):
    n_layers, d, _ = w_in.shape
    rw = GROUP_WIDTH

    def interleave_halves(w):
        w6 = w.reshape(n_layers, d, rw // LANES, PAIR, 2, HEAD_DIM // 2)
        return w6.transpose(0, 1, 2, 4, 3, 5).reshape(n_layers, d, rw)

    q, k = interleave_halves(w_in[:, :, :rw]), interleave_halves(w_in[:, :, rw:2 * rw])
    v, g = w_in[:, :, 2 * rw:3 * rw], w_in[:, :, 3 * rw:4 * rw]
    per_pair = [x[:, :, p * LANES:(p + 1) * LANES] for p in range(rw // LANES) for x in (q, k, v, g)]
    w = jnp.concatenate(per_pair + [w_in[:, :, 4 * rw:]], axis=2).astype(BF16)
    names = ["rr", "kr", "vr", "lora"]
    col = {nm: i * (rw // LANES) for i, nm in enumerate(names)}
    zeros = jnp.zeros_like(w_up)
    w_up_pad = jnp.concatenate([w_up, zeros], axis=1).astype(BF16)
    a_up_pad = jnp.concatenate([zeros, a_up], axis=1).astype(BF16)
    return w, col, w_up_pad, a_up_pad


def _swa_layout(w_qkv, b_qkv, n_heads=16, n_kv=4):
    qw = n_heads * HEAD_DIM
    kw = n_kv * HEAD_DIM
    group = qw // n_kv

    def layout(x):
        q, k, v = x[..., :qw], x[..., qw:qw + kw], x[..., qw + kw:]
        parts = []
        for hd in range(n_kv):
            k_h = k[..., hd * HEAD_DIM:(hd + 1) * HEAD_DIM]
            v_h = v[..., hd * HEAD_DIM:(hd + 1) * HEAD_DIM]
            parts += [q[..., hd * group:(hd + 1) * group], k_h, k_h, v_h, v_h]
        return jnp.concatenate(parts, axis=-1)

    return layout(w_qkv).astype(BF16), layout(b_qkv)[:, None, :]


def _rope_tables(seq, dtype):
    half = HEAD_DIM // 2
    inv_freq = ROPE_BASE ** (-jnp.linspace(0.0, 1.0, half, dtype=dtype))
    ang = jnp.arange(seq, dtype=dtype)[:, None] * inv_freq[None, :]
    sin = jnp.sin(ang)
    cos = jnp.cos(ang)
    return (jnp.tile(cos, (1, LANES // half)),
            jnp.concatenate([-sin] * PAIR + [sin] * PAIR, axis=1))


def kernel(x, norm1_g, norm2_g, final_g, even_w_in, even_w_out, rwkv_mu, rwkv_w0, rwkv_w_up,
           rwkv_a0, rwkv_a_up, rwkv_g_up, rwkv_k_k, rwkv_k_a, rwkv_r_k, rwkv_ln_g, rwkv_ln_b,
           swa_w_qkv, swa_b_qkv, swa_sinks, swa_w_o, swa_b_o, ffn_w_gate, ffn_w_up, ffn_w_down):
    batch, seq, d = x.shape
    depth = norm1_g.shape[0]
    h = x.reshape(batch * seq, d)
    cos, sin = _rope_tables(seq, x.dtype)
    sinks_flat = swa_sinks.reshape(-1)

    def rows(p):
        return p.reshape(p.shape[0], 1, -1)

    even_w, col, w_up_pad, a_up_pad = _even_layout(even_w_in, rwkv_w_up, rwkv_a_up)
    rwkv_params = dict(mu=rows(rwkv_mu), w0=rows(rwkv_w0), w_up=w_up_pad, a0=rows(rwkv_a0), a_up=a_up_pad,
                       g_up=rwkv_g_up.astype(BF16), k_k=rows(rwkv_k_k), k_a=rows(rwkv_k_a),
                       r_k=rows(rwkv_r_k), ln_g=rows(rwkv_ln_g), ln_b=rows(rwkv_ln_b))
    swa_w, swa_b = _swa_layout(swa_w_qkv, swa_b_qkv)
    g1, g2 = rows(norm1_g), rows(norm2_g)
    even_wo, swa_wo, swa_bo = even_w_out.astype(BF16), swa_w_o.astype(BF16), rows(swa_b_o)
    wg, wu, wd = ffn_w_gate.astype(BF16), ffn_w_up.astype(BF16), ffn_w_down.astype(BF16)

    for layer in range(depth):
        i = layer // 2
        if layer % 2 == 0:
            z, ret_out = _proj_retention(h, g1, layer, even_w, i, cos, sin, seq)
            mixes = [ret_out, _rwkv(z, rwkv_params, i, batch, seq, col)]
            wo, bo = even_wo, None
        else:
            mixes = [_proj_swa(h, g1, layer, swa_w, swa_b, i, sinks_flat, seq)]
            wo, bo = swa_wo, swa_bo
        h = _post(h, mixes, wo, i, bo, g2, wg, wu, wd, layer, final_g if layer == depth - 1 else None)
    return h.reshape(batch, seq, d)
```

```python
import functools

import jax
import jax.numpy as jnp
from jax import lax
from jax.experimental import pallas as pl
from jax.experimental.pallas import tpu as pltpu

F32 = jnp.float32
BF16 = jnp.bfloat16

LANES = 128
SUBLANES = 8
HEAD_DIM = 64
PAIR = LANES // HEAD_DIM
GROUP_WIDTH = 8 * HEAD_DIM
RMS_EPS = 1e-6
RET_GN_EPS = 1e-6
RWKV_GN_EPS = 64e-5
ROPE_BASE = 10000.0
RET_CHUNK = 128
RWKV_CHUNK = 64
SWA_WINDOW = 128
VMEM_LIMIT = 56 * 1024 * 1024


def _dot(a, b):
    return jnp.dot(a.astype(BF16), b.astype(BF16), preferred_element_type=F32)


def _dot_nt(a, b):
    return lax.dot_general(a.astype(BF16), b.astype(BF16), (((1,), (1,)), ((), ())),
                           preferred_element_type=F32)


def _split2(x):
    hi = x.astype(BF16)
    lo = (x - hi.astype(F32)).astype(BF16)
    return hi, lo


def _dot_hi2(x, m):
    hi, lo = _split2(x)
    return (jnp.dot(hi, m, preferred_element_type=F32)
            + jnp.dot(lo, m, preferred_element_type=F32))


def _dot_hi3_left(m, x):
    hi = x.astype(BF16)
    r1 = x - hi.astype(F32)
    mid = r1.astype(BF16)
    lo = (r1 - mid.astype(F32)).astype(BF16)
    return (jnp.dot(m, hi, preferred_element_type=F32)
            + jnp.dot(m, mid, preferred_element_type=F32)
            + jnp.dot(m, lo, preferred_element_type=F32))


def _iota2(shape, dim):
    return lax.broadcasted_iota(jnp.int32, shape, dim)


def _head_mask(rows):
    return _iota2((rows, LANES), 1) < HEAD_DIM


def _stack_heads(x, m0):
    zero = jnp.zeros_like(x)
    return jnp.concatenate([jnp.where(m0, x, zero), jnp.where(m0, zero, x)], axis=0)


def _unstack_heads(xs, m0):
    r = xs.shape[0] // 2
    return jnp.where(m0, xs[:r], xs[r:])


def _group_matrix(scale):
    r = _iota2((LANES, LANES), 0) // HEAD_DIM
    c = _iota2((LANES, LANES), 1) // HEAD_DIM
    return jnp.where(r == c, scale, 0.0).astype(BF16)


def _rms_norm(x, g):
    ms = jnp.mean(x * x, axis=-1, keepdims=True)
    return x * lax.rsqrt(ms + RMS_EPS) * g


def _layer_spec(x, layer, rows=None, row_block=0):
    _, r, c = x.shape
    return pl.BlockSpec((None, rows or r, c), lambda *_: (layer, row_block, 0),
                        pipeline_mode=pl.Buffered(1))


def _proj_retention_kernel(h_ref, g_ref, w_ref, cos_ref, sin_ref, dmask_ref, qdec_ref, kdec_ref, cdec_ref,
                           z_ref, o_ref, s_ref, *, n_pairs, blocks_per_seq, rest_stage_cols):
    @pl.when(pl.program_id(0) % blocks_per_seq == 0)
    def _():
        s_ref[...] = jnp.zeros_like(s_ref)

    n = _rms_norm(h_ref[...], g_ref[...]).astype(BF16)
    tm = n.shape[0]
    c = RET_CHUNK
    n_chunks = tm // c
    half = HEAD_DIM // 2
    m0 = _head_mask(c)
    m0_qk = (_iota2((c, LANES), 1) // half) % PAIR == 0
    gmean = _group_matrix(1.0 / HEAD_DIM)
    key_head = (_iota2((LANES, LANES), 0) // half) % PAIR
    value_head = _iota2((LANES, LANES), 1) // HEAD_DIM
    same_head = key_head == value_head
    pair_cols = 4 * LANES

    def project(c0, c1):
        return jnp.dot(n, w_ref[:, c0:c1], preferred_element_type=F32)

    def project_pair(p):
        parts = []
        for i in range(2):
            both = project(p * pair_cols + 2 * i * LANES, p * pair_cols + 2 * (i + 1) * LANES)
            parts += [both[:, :LANES], both[:, LANES:]]
            yield
        return parts

    def project_rest():
        base = n_pairs * pair_cols
        for c0 in range(0, z_ref.shape[1], rest_stage_cols):
            z_ref[:, c0:c0 + rest_stage_cols] = project(base + c0, base + c0 + rest_stage_cols)
            yield

    def swap_halves(x):
        return pltpu.roll(x, HEAD_DIM, axis=1)

    def retention_pair(p, zq, zk, zv, zg):
        dmask, qdec, kdec, cdec = dmask_ref[p], qdec_ref[p], kdec_ref[p], cdec_ref[p]

        def inner_terms(j):
            rows = slice(j * c, (j + 1) * c)
            cos = cos_ref[rows, :]
            sin = sin_ref[rows, :]
            q = zq[rows] * cos + swap_halves(zq[rows]) * sin
            k = (zk[rows] * cos + swap_halves(zk[rows]) * sin) * (HEAD_DIM ** -0.5)
            v = zv[rows].astype(BF16)
            scores = _dot_nt(_stack_heads(q, m0_qk), k) * dmask
            kv = jnp.where(same_head, _dot((k * kdec).T, v), 0.0)
            yield
            o_inner = _unstack_heads(_dot(scores, v), m0)
            return (q * qdec).astype(BF16), o_inner, kv

        inner = yield from _lockstep_stages(inner_terms(j) for j in range(n_chunks))
        state = s_ref[p]
        states = []
        for _, _, kv in inner:
            states.append(state)
            state = cdec * state + kv
        s_ref[p] = state

        def finish(j):
            rows = slice(j * c, (j + 1) * c)
            q_dec, o_inner, _ = inner[j]
            o = o_inner + _dot(q_dec, states[j])
            yield
            mu = _dot_hi2(o, gmean)
            yield
            d = o - mu
            var = _dot_hi2(d * d, gmean)
            yield
            g = zg[rows]
            out = d * lax.rsqrt(var + RET_GN_EPS) * (g * jax.nn.sigmoid(g))
            o_ref[rows, p * LANES:(p + 1) * LANES] = out.astype(o_ref.dtype)

        yield from _lockstep_stages(finish(j) for j in range(n_chunks))

    rest = project_rest()
    n_rest_stages = z_ref.shape[1] // rest_stage_cols

    def upcoming(p):
        parts = None
        if p + 1 < n_pairs:
            parts = yield from project_pair(p + 1)
        first = n_rest_stages * p // n_pairs
        last = n_rest_stages * (p + 1) // n_pairs
        for _ in range(first, last):
            next(rest)
            yield
        return parts

    projected = _lockstep([project_pair(0)])[0]
    for p in range(n_pairs):
        projected = _lockstep([retention_pair(p, *projected), upcoming(p)])[1]


def _retention_tables(dtype):
    c = RET_CHUNK
    n_heads = GROUP_WIDTH // HEAD_DIM
    h = jnp.arange(n_heads, dtype=dtype)
    log_gamma = jnp.log1p(-(2.0 ** (-5.0 - h)))
    idx = jnp.arange(c, dtype=dtype)
    rel = idx[:, None] - idx[None, :]
    inner = jnp.where(rel >= 0, jnp.exp(jnp.maximum(rel, 0.0)[None] * log_gamma[:, None, None]), 0.0)
    dmask = inner.reshape(n_heads // PAIR, PAIR * c, c)
    pair_lg = log_gamma.reshape(n_heads // PAIR, PAIR)
    lanes_lg = jnp.repeat(pair_lg, HEAD_DIM, axis=1)
    qk_lg = jnp.tile(jnp.repeat(pair_lg, HEAD_DIM // 2, axis=1), (1, 2))
    qdec = jnp.exp((idx + 1.0)[None, :, None] * qk_lg[:, None, :])
    kdec = jnp.exp((c - 1 - idx)[None, :, None] * qk_lg[:, None, :])
    cdec = jnp.broadcast_to(jnp.exp(c * lanes_lg)[:, None, :], (n_heads // PAIR, LANES, LANES))
    return dmask, qdec, kdec, cdec


def _proj_retention(h2, g_all, layer, w_all, sub, cos, sin, seq, n_pairs=4, tm=1024):
    m, d = h2.shape
    n_rest = w_all.shape[2] - n_pairs * 4 * LANES
    blocks_per_seq = seq // tm
    tables = _retention_tables(h2.dtype)
    whole = lambda x: pl.BlockSpec(x.shape, lambda i: (0,) * x.ndim, pipeline_mode=pl.Buffered(1))
    rope = pl.BlockSpec((tm, LANES), lambda i: (i % blocks_per_seq, 0))
    return pl.pallas_call(
        functools.partial(_proj_retention_kernel, n_pairs=n_pairs, blocks_per_seq=blocks_per_seq,
                          rest_stage_cols=2 * LANES),
        grid=(m // tm,),
        in_specs=[pl.BlockSpec((tm, d), lambda i: (i, 0)), _layer_spec(g_all, layer), _layer_spec(w_all, sub),
                  rope, rope] + [whole(t) for t in tables],
        out_specs=[pl.BlockSpec((tm, n_rest), lambda i: (i, 0)),
                   pl.BlockSpec((tm, n_pairs * LANES), lambda i: (i, 0))],
        out_shape=[jax.ShapeDtypeStruct((m, n_rest), F32),
                   jax.ShapeDtypeStruct((m, n_pairs * LANES), BF16)],
        scratch_shapes=[pltpu.VMEM((n_pairs, LANES, LANES), F32)],
        compiler_params=pltpu.CompilerParams(dimension_semantics=("arbitrary",),
                                             vmem_limit_bytes=VMEM_LIMIT),
        name="proj_retention",
    )(h2, g_all, w_all, cos, sin, *tables)


def _lockstep_stages(gens):
    gens = list(gens)
    results = [None] * len(gens)
    live = list(range(len(gens)))
    while live:
        still = []
        for i in live:
            try:
                next(gens[i])
                still.append(i)
            except StopIteration as stop:
                results[i] = stop.value
        live = still
        if live:
            yield
    return results


def _lockstep(gens):
    stages = _lockstep_stages(gens)
    while True:
        try:
            next(stages)
        except StopIteration as stop:
            return stop.value


def _inverse_masks():
    r = _iota2((LANES, LANES), 0)
    c = _iota2((LANES, LANES), 1)
    eye = jnp.where(r == c, 1.0, 0.0).astype(F32)
    diag8 = r // 8 == c // 8
    lower_left = [(r // (2 * k) == c // (2 * k)) & (r // k > c // k) for k in (8, 16, 32)]
    return eye, diag8, lower_left


def _unit_lower_inverse(n_mat, masks):
    eye, diag8, lower_left = masks
    p = jnp.where(diag8, n_mat, 0.0)
    t = eye + p
    p = _dot(p, p)
    yield
    tp = _dot(jnp.concatenate([t, p], axis=0), p)
    t = t + tp[:LANES]
    p = tp[LANES:]
    yield
    t = t + _dot(t, p)
    yield
    for mask in lower_left:
        lt = _dot(jnp.where(mask, n_mat, 0.0), t)
        yield
        t = t + _dot(t, lt)
        yield
    return t


def _delayed(gen, rounds):
    for _ in range(rounds):
        yield
    yield from gen


def _rwkv_kernel(fr_ref, fk_ref, fv_ref, fl_ref, nr_ref, nk_ref, nv_ref, nl_ref,
                 mur_ref, muk_ref, muv_ref, mul_ref,
                 w0_ref, wup_ref, a0_ref, aup_ref, gup_ref, kk_ref, ka_ref, rk_ref, lng_ref, lnb_ref,
                 o_ref,
                 ht_ref, prev_ref, prevl_ref, r_s, lw_s, k_s, v_s, a_s, b_s, bonus_s, gate_s,
                 rhat_s, yhat_s, m_s, g_s, gtot_s, y_s, *, tb):
    t_idx = pl.program_id(2)
    c = RWKV_CHUNK
    n_chunks = tb // c
    gsum = _group_matrix(1.0)
    gmean = _group_matrix(1.0 / HEAD_DIM)
    pr = 2 * c
    first_row = _iota2((pr, 1), 0) == 0

    def prepare(zr_ref, zk_ref, zv_ref, zl_ref, slot):
        def prepare_rows(j):
            rows = slice(j * pr, (j + 1) * pr)

            def mixed(z_ref, mu_ref, carry_ref, carry_row):
                z = z_ref[rows, :]
                before = carry_ref[carry_row:carry_row + 1, :] if j == 0 else z_ref[j * pr - 1:j * pr, :]
                z_prev = jnp.where(first_row, before, pltpu.roll(z, 1, axis=0))
                return z + mu_ref[...] * (z_prev - z)

            xr = mixed(zr_ref, mur_ref, prev_ref, 0)
            xk = mixed(zk_ref, muk_ref, prev_ref, 1)
            xv = mixed(zv_ref, muv_ref, prev_ref, 2)
            xl = mixed(zl_ref, mul_ref, prevl_ref, 0)
            x_wa = xl[:, :LANES]
            w_pre = _dot(jnp.tanh(x_wa), wup_ref[...])
            a_pre = _dot(x_wa, aup_ref[...])
            gate = _dot(jax.nn.sigmoid(xl[:, LANES:]), gup_ref[...])
            kkf = xk * kk_ref[...]
            norm2 = _dot_hi2(kkf * kkf, gsum)
            yield
            w_log = -jax.nn.softplus(-(w0_ref[...] + w_pre)) - 0.5
            a = jax.nn.sigmoid(a0_ref[...] + a_pre)
            kk = kkf / jnp.maximum(jnp.sqrt(norm2), 1e-12)
            k2 = xk * (1.0 + (a - 1.0) * ka_ref[...])
            rk_sum = _dot_hi2(xr * k2 * rk_ref[...], gsum)
            gate_s[slot, rows, :] = gate
            r_s[rows, :] = xr
            lw_s[rows, :] = -jnp.exp(w_log)
            k_s[rows, :] = k2
            v_s[rows, :] = xv
            a_s[rows, :] = -kk
            b_s[rows, :] = kk * a
            yield
            bonus_s[slot, rows, :] = rk_sum * xv

        waiting = [prepare_rows(j) for j in range(tb // pr)]
        running = []
        while waiting or running:
            if waiting:
                running.append(waiting.pop(0))
            running = [g for g in running if next(g, "done") != "done"]
            yield
        prev_ref[0:1, :] = zr_ref[tb - 1:tb, :]
        prev_ref[1:2, :] = zk_ref[tb - 1:tb, :]
        prev_ref[2:3, :] = zv_ref[tb - 1:tb, :]
        prevl_ref[0:1, :] = zl_ref[tb - 1:tb, :]

    @pl.when(t_idx == 0)
    def _():
        for ref in (ht_ref, prev_ref, prevl_ref, bonus_s, gate_s, rhat_s, yhat_s, m_s, g_s, gtot_s):
            ref[...] = jnp.zeros_like(ref)
        for _ in prepare(fr_ref, fk_ref, fv_ref, fl_ref, 0):
            pass

    slot_next = lax.rem(t_idx + 1, 3)
    slot_done = lax.rem(t_idx + 2, 3)

    m0 = _head_mask(c)
    ri = _iota2((LANES, LANES), 0)
    ci = _iota2((LANES, LANES), 1)
    same_head = ri // c == ci // c
    strict = same_head & (ri > ci)
    incl = same_head & (ri >= ci)
    tri = jnp.where(_iota2((c, c), 0) >= _iota2((c, c), 1), 1.0, 0.0).astype(BF16)

    inv_masks = _inverse_masks()

    def chunk_terms(j):
        rows = slice(j * c, (j + 1) * c)
        lw = lw_s[rows, :]
        cum = _dot_hi3_left(tri, lw)
        yield
        cum_last = cum[c - 1:c, :]
        g_inv = jnp.exp(-cum)
        g_rem = jnp.exp(cum_last - cum)
        a_st = _stack_heads(a_s[rows, :] * jnp.exp(cum - lw), m0)
        r_st = _stack_heads(r_s[rows, :] * jnp.exp(cum), m0)
        v_st = _stack_heads(v_s[rows, :], m0)
        bg_st = _stack_heads(b_s[rows, :] * g_rem, m0)
        kg_st = _stack_heads(k_s[rows, :] * g_rem, m0)
        b_t = (b_s[rows, :] * g_inv).astype(BF16)
        k_t = (k_s[rows, :] * g_inv).astype(BF16)
        b2 = jnp.concatenate([b_t, b_t], axis=0)
        k2_ = jnp.concatenate([k_t, k_t], axis=0)
        ar_bf = jnp.concatenate([a_st, r_st], axis=0).astype(BF16)
        v_bf = v_st.astype(BF16)
        ar_b = _dot_nt(ar_bf, b2)
        ar_k = _dot_nt(ar_bf, k2_)
        n_ab = jnp.where(strict, ar_b[:LANES], 0.0)
        n_ak = jnp.where(strict, ar_k[:LANES], 0.0)
        n_rb = jnp.where(incl, ar_b[LANES:], 0.0)
        n_rk = jnp.where(incl, ar_k[LANES:], 0.0)
        yield
        akv = _dot(n_ak, v_bf)
        bk_t = jnp.concatenate([bg_st, kg_st], axis=0).T.astype(BF16)
        t_inv = yield from _unit_lower_inverse(n_ab, inv_masks)
        wu = _dot(t_inv, jnp.concatenate([a_st, akv], axis=1))
        yield
        wu_v = jnp.concatenate(
            [wu.astype(BF16), jnp.concatenate([jnp.zeros_like(v_bf), v_bf], axis=1)], axis=0)
        lhs = jnp.concatenate([jnp.concatenate([n_rb, n_rk], axis=1).astype(BF16), bk_t], axis=0)
        both = jnp.dot(lhs, wu_v, preferred_element_type=F32)
        rw = both[:LANES]
        mg_t = both[LANES:]
        yield
        return ((r_st + rw[:, :LANES]).astype(BF16), rw[:, LANES:], mg_t[:, :LANES].astype(BF16),
                mg_t[:, LANES:].T, jnp.exp(cum_last))

    def chain_and_output():
        ht = ht_ref[...]
        for j in range(n_chunks):
            y_st = _dot_nt(rhat_s[j], ht) + yhat_s[j]
            y_s[j * c:(j + 1) * c, :] = y_st[:c] + y_st[c:]
            ht = ht * gtot_s[SUBLANES * j:SUBLANES * j + 1, :] + _dot_nt(ht, m_s[j]) + g_s[j]
            yield
        ht_ref[...] = ht
        y = y_s[...]
        mu = _dot_hi2(y, gmean)
        yield
        d = y - mu
        var = _dot_hi2(d * d, gmean)
        yield
        yn = d * lax.rsqrt(var + RWKV_GN_EPS) * lng_ref[...] + lnb_ref[...]
        o_ref[...] = ((yn + bonus_s[slot_done]) * gate_s[slot_done]).astype(o_ref.dtype)

    done = _lockstep([chain_and_output(), _delayed(prepare(nr_ref, nk_ref, nv_ref, nl_ref, slot_next), 2)]
                     + [chunk_terms(j) for j in range(n_chunks)])
    for j, (r_hat, y_hat, m_t, g_mat, g_tot) in enumerate(done[2:]):
        rhat_s[j] = r_hat
        yhat_s[j] = y_hat
        m_s[j] = m_t
        g_s[j] = g_mat
        gtot_s[SUBLANES * j:SUBLANES * j + 1, :] = g_tot


def _rwkv(z, params, layer, batch, seq, col, tb=1024):
    nt = seq // tb
    n_pairs = 4

    def zspec(name, block_of_step, width=LANES):
        off = col[name]
        if width == LANES:
            return pl.BlockSpec((tb, LANES), lambda b, p, t: (b * nt + block_of_step(t), off + p))
        return pl.BlockSpec((tb, width), lambda b, p, t: (b * nt + block_of_step(t), off * LANES // width))

    first_block = lambda t: 0
    next_block = lambda t: jnp.minimum(t + 1, nt - 1)

    def pair_row(x, first_block=0):
        return x, pl.BlockSpec((None, 1, LANES), lambda b, p, t: (layer, 0, first_block + p))

    def pair_cols(x):
        return x, pl.BlockSpec((None, x.shape[1], LANES), lambda b, p, t: (layer, 0, p))

    mu = params["mu"]
    rw_blocks = GROUP_WIDTH // LANES
    mu_lora = (mu, pl.BlockSpec((None, 1, 2 * LANES), lambda b, p, t: (layer, 0, 3 * rw_blocks // 2)))

    args, specs = [z] * 8, []
    for block_of_step in (first_block, next_block):
        specs += [zspec("rr", block_of_step), zspec("kr", block_of_step), zspec("vr", block_of_step),
                  zspec("lora", block_of_step, 2 * LANES)]
    for a, s in (pair_row(mu, 0), pair_row(mu, rw_blocks), pair_row(mu, 2 * rw_blocks), mu_lora,
                 pair_row(params["w0"]), pair_cols(params["w_up"]),
                 pair_row(params["a0"]), pair_cols(params["a_up"]), pair_cols(params["g_up"]),
                 pair_row(params["k_k"]), pair_row(params["k_a"]), pair_row(params["r_k"]),
                 pair_row(params["ln_g"]), pair_row(params["ln_b"])):
        args.append(a)
        specs.append(s)

    n_chunks = tb // RWKV_CHUNK
    blk = pltpu.VMEM((tb, LANES), F32)
    ring = pltpu.VMEM((3, tb, LANES), F32)
    mat = lambda dtype: pltpu.VMEM((n_chunks, LANES, LANES), dtype)
    return pl.pallas_call(
        functools.partial(_rwkv_kernel, tb=tb),
        grid=(batch, n_pairs, nt + 1),
        in_specs=specs,
        out_specs=pl.BlockSpec((tb, LANES), lambda b, p, t: (b * nt + jnp.maximum(t - 1, 0), p)),
        out_shape=jax.ShapeDtypeStruct((batch * seq, n_pairs * LANES), BF16),
        scratch_shapes=[pltpu.VMEM((LANES, LANES), F32),
                        pltpu.VMEM((SUBLANES, LANES), F32),
                        pltpu.VMEM((SUBLANES, 2 * LANES), F32),
                        blk, blk, blk, blk, blk, blk,
                        ring, ring,
                        mat(BF16), mat(F32), mat(BF16), mat(F32),
                        pltpu.VMEM((SUBLANES * n_chunks, LANES), F32),
                        blk],
        compiler_params=pltpu.CompilerParams(
            dimension_semantics=("parallel", "parallel", "arbitrary"),
            vmem_limit_bytes=VMEM_LIMIT),
        name="rwkv7",
    )(*args)


def _proj_swa_kernel(sink_ref, h_ref, g_ref, w_ref, b_ref, o_ref, carry_ref, *,
                     n_kv, blocks_per_seq, layer_sink_base):
    starts_sequence = pl.program_id(0) % blocks_per_seq == 0

    @pl.when(starts_sequence)
    def _():
        carry_ref[...] = jnp.zeros_like(carry_ref)

    n = _rms_norm(h_ref[...], g_ref[...]).astype(BF16)
    tm = n.shape[0]
    w = SWA_WINDOW
    n_windows = tm // w
    m0 = _head_mask(w)
    keep_head0 = jnp.where(m0, 1.0, 0.0).astype(BF16)
    keep_head1 = jnp.where(m0, 0.0, 1.0).astype(BF16)
    from_prev = _iota2((PAIR * w, w), 1) > _iota2((PAIR * w, w), 0) % w
    first_bias = jnp.where(starts_sequence, -jnp.inf, 0.0).astype(F32)
    second_head = _iota2((PAIR * w, 1), 0) >= w
    head_cols = 4 * LANES

    def project_head(hd):
        parts = []
        for i in range(2):
            cols = slice(hd * head_cols + 2 * i * LANES, hd * head_cols + 2 * (i + 1) * LANES)
            z = jnp.dot(n, w_ref[:, cols], preferred_element_type=F32) + b_ref[:, cols]
            parts.append(z.astype(BF16))
            yield
        return parts

    def attend(hd, win, j, q4, kv):
        qb = 2 * hd + j
        rows = slice(win * w, (win + 1) * w)
        if win == 0:
            k_prev, v_prev = carry_ref[:, 2 * hd * LANES:(2 * hd + 1) * LANES], carry_ref[:, (2 * hd + 1) * LANES:(2 * hd + 2) * LANES]
            prev_bias = first_bias
        else:
            before = slice((win - 1) * w, win * w)
            k_prev, v_prev = kv[before, :LANES], kv[before, LANES:]
            prev_bias = 0.0
        q = q4[rows, j * LANES:(j + 1) * LANES] * (HEAD_DIM ** -0.5)
        kd = jnp.concatenate([k_prev, kv[rows, :LANES]], axis=0)
        vd = jnp.concatenate([v_prev, kv[rows, LANES:]], axis=0)
        q_stacked = jnp.concatenate([q * keep_head0, q * keep_head1], axis=0)
        s = _dot_nt(q_stacked, kd)
        yield
        sm = jnp.where(from_prev, s[:, :w] + prev_bias, s[:, w:])
        sink = jnp.where(second_head, sink_ref[layer_sink_base + 2 * qb + 1],
                         sink_ref[layer_sink_base + 2 * qb])
        m = jnp.maximum(jnp.max(sm, axis=-1, keepdims=True), sink)
        e = jnp.exp(sm - m)
        denom = jnp.sum(e, axis=-1, keepdims=True) + jnp.exp(sink - m)
        e_split = jnp.concatenate([jnp.where(from_prev, e, 0.0).astype(BF16),
                                   jnp.where(from_prev, 0.0, e).astype(BF16)], axis=1)
        pv = _dot(e_split, vd)
        yield
        o_ref[rows, qb * LANES:(qb + 1) * LANES] = _unstack_heads(pv / denom, m0).astype(o_ref.dtype)

    projected = _lockstep([project_head(0)])[0]
    for hd in range(n_kv):
        work = [attend(hd, win, j, *projected) for win in range(n_windows) for j in range(2)]
        if hd + 1 < n_kv:
            work.append(project_head(hd + 1))
        done = _lockstep(work)
        carry_ref[:, 2 * hd * LANES:(2 * hd + 2) * LANES] = projected[1][tm - w:tm, :]
        projected = done[-1]


def _proj_swa(h2, g_all, layer, w_all, b_all, sub, sinks_flat, seq, n_heads=16, n_kv=4, tm=1024):
    m, d = h2.shape
    qw = n_heads * HEAD_DIM
    blocks_per_seq = seq // tm
    return pl.pallas_call(
        functools.partial(_proj_swa_kernel, n_kv=n_kv, blocks_per_seq=blocks_per_seq,
                          layer_sink_base=sub * n_heads),
        grid=(m // tm,),
        in_specs=[pl.BlockSpec(memory_space=pltpu.SMEM),
                  pl.BlockSpec((tm, d), lambda i: (i, 0)), _layer_spec(g_all, layer),
                  _layer_spec(w_all, sub), _layer_spec(b_all, sub)],
        out_specs=pl.BlockSpec((tm, qw), lambda i: (i, 0)),
        out_shape=jax.ShapeDtypeStruct((m, qw), BF16),
        scratch_shapes=[pltpu.VMEM((SWA_WINDOW, n_kv * PAIR * LANES), BF16)],
        compiler_params=pltpu.CompilerParams(dimension_semantics=("arbitrary",),
                                             vmem_limit_bytes=VMEM_LIMIT),
        name="proj_swa",
    )(sinks_flat, h2, g_all, w_all, b_all)


def _post_kernel(*refs, n_mix, has_bias, has_final):
    it = iter(refs)
    h_ref = next(it)
    mix_refs = [next(it) for _ in range(n_mix)]
    wo_refs = [next(it) for _ in range(n_mix)]
    bo_ref = next(it) if has_bias else None
    g2_ref, wg_ref, wu_ref, wd_ref = next(it), next(it), next(it), next(it)
    gf_ref = next(it) if has_final else None
    o_ref = next(it)

    h = h_ref[...]
    for m_ref, w_ref in zip(mix_refs, wo_refs):
        h = h + jnp.dot(m_ref[...].astype(BF16), w_ref[...], preferred_element_type=F32)
    if has_bias:
        h = h + bo_ref[...]
    n = _rms_norm(h, g2_ref[...]).astype(BF16)
    gate = jnp.dot(n, wg_ref[...], preferred_element_type=F32)
    up = jnp.dot(n, wu_ref[...], preferred_element_type=F32)
    act = (gate * jax.nn.sigmoid(gate) * up).astype(BF16)
    h = h + jnp.dot(act, wd_ref[...], preferred_element_type=F32)
    if has_final:
        h = _rms_norm(h, gf_ref[...])
    o_ref[...] = h


def _post(h2, mixes, wo_all, sub, bo_all, g2_all, wg_all, wu_all, wd_all, layer, final_g, tm=512):
    m, d = h2.shape
    row = lambda width: pl.BlockSpec((tm, width), lambda i: (i, 0))
    args, specs = [h2], [row(d)]
    for mx in mixes:
        args.append(mx)
        specs.append(row(mx.shape[1]))
    for k, mx in enumerate(mixes):
        args.append(wo_all)
        specs.append(_layer_spec(wo_all, sub, rows=mx.shape[1], row_block=k))
    if bo_all is not None:
        args.append(bo_all)
        specs.append(_layer_spec(bo_all, sub))
    for a in (g2_all, wg_all, wu_all, wd_all):
        args.append(a)
        specs.append(_layer_spec(a, layer))
    if final_g is not None:
        args.append(final_g.reshape(1, d))
        specs.append(pl.BlockSpec((1, d), lambda i: (0, 0), pipeline_mode=pl.Buffered(1)))
    bo = bo_all
    return pl.pallas_call(
        functools.partial(_post_kernel, n_mix=len(mixes), has_bias=bo is not None,
                          has_final=final_g is not None),
        grid=(m // tm,),
        in_specs=specs,
        out_specs=row(d),
        out_shape=jax.ShapeDtypeStruct((m, d), F32),
        compiler_params=pltpu.CompilerParams(dimension_semantics=("parallel",),
                                             vmem_limit_bytes=VMEM_LIMIT),
        name="post_ffn",
    )(*args)


def _even_layout(w_in, w_up, a_up):
    n_layers, d, _ = w_in.shape
    rw = GROUP_WIDTH

    def interleave_halves(w):
        w6 = w.reshape(n_layers, d, rw // LANES, PAIR, 2, HEAD_DIM // 2)
        return w6.transpose(0, 1, 2, 4, 3, 5).reshape(n_layers, d, rw)

    q, k = interleave_halves(w_in[:, :, :rw]), interleave_halves(w_in[:, :, rw:2 * rw])
    v, g = w_in[:, :, 2 * rw:3 * rw], w_in[:, :, 3 * rw:4 * rw]
    per_pair = [x[:, :, p * LANES:(p + 1) * LANES] for p in range(rw // LANES) for x in (q, k, v, g)]
    w = jnp.concatenate(per_pair + [w_in[:, :, 4 * rw:]], axis=2).astype(BF16)
    names = ["rr", "kr", "vr", "lora"]
    col = {nm: i * (rw // LANES) for i, nm in enumerate(names)}
    zeros = jnp.zeros_like(w_up)
    w_up_pad = jnp.concatenate([w_up, zeros], axis=1).astype(BF16)
    a_up_pad = jnp.concatenate([zeros, a_up], axis=1).astype(BF16)
    return w, col, w_up_pad, a_up_pad


def _swa_layout(w_qkv, b_qkv, n_heads=16, n_kv=4):
    qw = n_heads * HEAD_DIM
    kw = n_kv * HEAD_DIM
    group = qw // n_kv

    def layout(x):
        q, k, v = x[..., :qw], x[..., qw:qw + kw], x[..., qw + kw:]
        parts = []
        for hd in range(n_kv):
            k_h = k[..., hd * HEAD_DIM:(hd + 1) * HEAD_DIM]
            v_h = v[..., hd * HEAD_DIM:(hd + 1) * HEAD_DIM]
            parts += [q[..., hd * group:(hd + 1) * group], k_h, k_h, v_h, v_h]
        return jnp.concatenate(parts, axis=-1)

    return layout(w_qkv).astype(BF16), layout(b_qkv)[:, None, :]


def _rope_tables(seq, dtype):
    half = HEAD_DIM // 2
    inv_freq = ROPE_BASE ** (-jnp.linspace(0.0, 1.0, half, dtype=dtype))
    ang = jnp.arange(seq, dtype=dtype)[:, None] * inv_freq[None, :]
    sin = jnp.sin(ang)
    cos = jnp.cos(ang)
    return (jnp.tile(cos, (1, LANES // half)),
            jnp.concatenate([-sin] * PAIR + [sin] * PAIR, axis=1))


def kernel(x, norm1_g, norm2_g, final_g, even_w_in, even_w_out, rwkv_mu, rwkv_w0, rwkv_w_up,
           rwkv_a0, rwkv_a_up, rwkv_g_up, rwkv_k_k, rwkv_k_a, rwkv_r_k, rwkv_ln_g, rwkv_ln_b,
           swa_w_qkv, swa_b_qkv, swa_sinks, swa_w_o, swa_b_o, ffn_w_gate, ffn_w_up, ffn_w_down):
    batch, seq, d = x.shape
    depth = norm1_g.shape[0]
    h = x.reshape(batch * seq, d)
    cos, sin = _rope_tables(seq, x.dtype)
    sinks_flat = swa_sinks.reshape(-1)

    def rows(p):
        return p.reshape(p.shape[0], 1, -1)

    even_w, col, w_up_pad, a_up_pad = _even_layout(even_w_in, rwkv_w_up, rwkv_a_up)
    rwkv_params = dict(mu=rows(rwkv_mu), w0=rows(rwkv_w0), w_up=w_up_pad, a0=rows(rwkv_a0), a_up=a_up_pad,
                       g_up=rwkv_g_up.astype(BF16), k_k=rows(rwkv_k_k), k_a=rows(rwkv_k_a),
                       r_k=rows(rwkv_r_k), ln_g=rows(rwkv_ln_g), ln_b=rows(rwkv_ln_b))
    swa_w, swa_b = _swa_layout(swa_w_qkv, swa_b_qkv)
    g1, g2 = rows(norm1_g), rows(norm2_g)
    even_wo, swa_wo, swa_bo = even_w_out.astype(BF16), swa_w_o.astype(BF16), rows(swa_b_o)
    wg, wu, wd = ffn_w_gate.astype(BF16), ffn_w_up.astype(BF16), ffn_w_down.astype(BF16)

    for layer in range(depth):
        i = layer // 2
        if layer % 2 == 0:
            z, ret_out = _proj_retention(h, g1, layer, even_w, i, cos, sin, seq)
            mixes = [ret_out, _rwkv(z, rwkv_params, i, batch, seq, col)]
            wo, bo = even_wo, None
        else:
            mixes = [_proj_swa(h, g1, layer, swa_w, swa_b, i, sinks_flat, seq)]
            wo, bo = swa_wo, swa_bo
        h = _post(h, mixes, wo, i, bo, g2, wg, wu, wd, layer, final_g if layer == depth - 1 else None)
    return h.reshape(batch, seq, d)
```
